```python
import math
import jax, jax.numpy as jnp
from jax import lax
import numpy as np

D_MODEL = 4096
BATCH = 4
SEQ = 2048
DEPTH = 4
DEC_BATCH = 128
DEC_SEQ = 4
PAST_LEN = 8192
PAGE_SIZE = 128

N_HEADS = 32
Q_LORA = 1024
KV_LORA = 512
NOPE_DIM = 128
ROPE_DIM = 64
V_DIM = 128
QK_DIM = NOPE_DIM + ROPE_DIM
SM_SCALE = 1.0 / math.sqrt(QK_DIM)
ROPE_THETA = 10000.0
Q_BLOCK = 128
SSM_EXPAND = 2
D_INNER = SSM_EXPAND * D_MODEL
SSM_HEAD_DIM = 64
SSM_HEADS = D_INNER // SSM_HEAD_DIM
SSM_GROUPS = 8
SSM_STATE = 128
SSM_CONV = 4
SSM_CHUNK = 128
CONV_DIM = D_INNER + 2 * SSM_GROUPS * SSM_STATE
SSM_IN_DIM = D_INNER + CONV_DIM + SSM_HEADS
D_FF = 256 * ((8 * D_MODEL // 3 + 255) // 256)
FFN_CONV = 3
EPS = 1e-6
N_MLA = (DEPTH + 1) // 2
N_SSM = DEPTH // 2

kernel_name = 'hybrid_mla_mamba2_convffn_step'


def rms_norm(x, g):
    xf = x.astype(jnp.float32)
    y = xf * lax.rsqrt(jnp.mean(xf * xf, axis=-1, keepdims=True) + EPS)
    return (y * g.astype(jnp.float32)).astype(x.dtype)


def apply_rope(x, pos):
    half = ROPE_DIM // 2
    inv = jnp.exp(-math.log(ROPE_THETA) * jnp.arange(half, dtype=jnp.float32) / half)
    ang = pos.astype(jnp.float32)[:, None] * inv[None, :]
    shape = (ang.shape[0],) + (1,) * (x.ndim - 3) + (half,)
    cos = jnp.cos(ang).reshape(shape)
    sin = jnp.sin(ang).reshape(shape)
    xf = x.astype(jnp.float32)
    x1, x2 = xf[..., :half], xf[..., half:]
    return jnp.concatenate([x1 * cos - x2 * sin, x1 * sin + x2 * cos], axis=-1).astype(x.dtype)


def causal_dwconv(x, hist, w, bias):
    t = x.shape[1]
    xp = jnp.concatenate([hist.astype(x.dtype), x], axis=1)
    y = bias
    for j in range(w.shape[0]):
        y = y + xp[:, j:j + t] * w[j]
    return y, xp[:, t:]


def mla_project(h, pos, w_in, g_qa, g_kva, w_uq, w_uk, g_qn, g_qr, g_kr):
    b, t, _ = h.shape
    a = h @ w_in
    q_a = rms_norm(a[..., :Q_LORA], g_qa)
    c = rms_norm(a[..., Q_LORA:Q_LORA + KV_LORA], g_kva)
    k_r = apply_rope(rms_norm(a[..., Q_LORA + KV_LORA:], g_kr), pos)
    q = (q_a @ w_uq).reshape(b, t, N_HEADS, QK_DIM)
    q_n = rms_norm(q[..., :NOPE_DIM], g_qn)
    q_r = apply_rope(rms_norm(q[..., NOPE_DIM:], g_qr), pos)
    k_n = jnp.einsum('btc,chd->bthd', c, w_uk)
    k_nf = k_n.astype(jnp.float32)
    k_scale = lax.rsqrt(jnp.mean(k_nf * k_nf, axis=-1) + EPS).astype(h.dtype)
    return q_n, q_r, c, k_r, k_n, k_scale


def mla_prompt(h, pos, p):
    w_in, g_qa, g_kva, w_uq, w_uk, w_uv, w_out, g_qn, g_qr, g_kn, g_kr = p
    b, s, _ = h.shape
    q_n, q_r, c, k_r, k_n, k_scale = mla_project(h, pos, w_in, g_qa, g_kva, w_uq, w_uk, g_qn, g_qr, g_kr)
    k_hat = k_n * k_scale[..., None] * g_kn
    v = jnp.einsum('bsc,chd->bshd', c, w_uv)
    nb = s // Q_BLOCK
    qn_b = q_n.reshape(b, nb, Q_BLOCK, N_HEADS, NOPE_DIM).swapaxes(0, 1)
    qr_b = q_r.reshape(b, nb, Q_BLOCK, N_HEADS, ROPE_DIM).swapaxes(0, 1)
    k_pos = jnp.arange(s)

    def block(args):
        qn, qr, i = args
        sc = (jnp.einsum('bqhd,bkhd->bhqk', qn, k_hat).astype(jnp.float32)
              + jnp.einsum('bqhr,bkr->bhqk', qr, k_r).astype(jnp.float32)) * SM_SCALE
        q_pos = i * Q_BLOCK + jnp.arange(Q_BLOCK)
        sc = jnp.where(q_pos[:, None] >= k_pos[None, :], sc, -jnp.inf)
        pr = jax.nn.softmax(sc, axis=-1).astype(v.dtype)
        return jnp.einsum('bhqk,bkhd->bqhd', pr, v)

    o = lax.map(block, (qn_b, qr_b, jnp.arange(nb)))
    o = o.swapaxes(0, 1).reshape(b, s, N_HEADS * V_DIM)
    return o @ w_out, c, k_r, k_scale


def mla_sample(h, pos, cache_lat, cache_kr, cache_ks, layer, page_table, p):
    w_in, g_qa, g_kva, w_uq, w_uk, w_uv, w_out, g_qn, g_qr, g_kn, g_kr = p
    b, t, _ = h.shape
    q_n, q_r, c, k_r, k_n, k_scale = mla_project(h, pos, w_in, g_qa, g_kva, w_uq, w_uk, g_qn, g_qr, g_kr)
    q_lat = jnp.einsum('bthd,chd->bthc', q_n * g_kn, w_uk)

    def scores(lat, kr, ks):
        s_n = jnp.einsum('bthc,buc->bthu', q_lat, lat).astype(jnp.float32)
        s_r = jnp.einsum('bthr,bur->bthu', q_r, kr).astype(jnp.float32)
        return (s_n * jnp.swapaxes(ks, 1, 2)[:, None].astype(jnp.float32) + s_r) * SM_SCALE

    sc = scores(c, k_r, k_scale)
    causal = jnp.arange(t)[:, None] >= jnp.arange(t)[None, :]
    sc = jnp.where(causal[None, :, None, :], sc, -jnp.inf)
    m0 = jnp.max(sc, axis=-1)
    pr = jnp.exp(sc - m0[..., None])
    l0 = jnp.sum(pr, axis=-1)
    o0 = jnp.einsum('bthu,buc->bthc', pr, c.astype(jnp.float32))

    def step(carry, phys):
        m, l, o = carry
        lat = cache_lat[layer, phys]
        kr = cache_kr[layer, phys]
        ks = cache_ks[layer, phys]
        s_ = scores(lat, kr, ks)
        m_new = jnp.maximum(m, jnp.max(s_, axis=-1))
        alpha = jnp.exp(m - m_new)
        pp = jnp.exp(s_ - m_new[..., None])
        l = l * alpha + jnp.sum(pp, axis=-1)
        o = o * alpha[..., None] + jnp.einsum('bthu,buc->bthc', pp, lat.astype(jnp.float32))
        return (m_new, l, o), None

    (m, l, o), _ = lax.scan(step, (m0, l0, o0), page_table.T)
    o_lat = (o / l[..., None]).astype(h.dtype)
    out = jnp.einsum('bthc,chd->bthd', o_lat, w_uv).reshape(b, t, N_HEADS * V_DIM)
    return out @ w_out, c, k_r, k_scale


def ssd(x, dt, a, bm, cm, h0, chunk):
    b, t, nh, hp = x.shape
    g, n = bm.shape[2], bm.shape[3]
    hg = nh // g
    nc = t // chunk
    f32 = jnp.float32
    xdt = (x.astype(f32) * dt[..., None]).reshape(b, nc, chunk, g, hg, hp)
    cum = jnp.cumsum((dt * a).reshape(b, nc, chunk, g, hg), axis=2)
    bm = bm.astype(f32).reshape(b, nc, chunk, g, n)
    cm = cm.astype(f32).reshape(b, nc, chunk, g, n)
    seg = cum[:, :, :, None] - cum[:, :, None, :]
    tri = jnp.tril(jnp.ones((chunk, chunk), dtype=bool))
    decay = jnp.exp(jnp.where(tri[:, :, None, None], seg, -jnp.inf))
    cb = jnp.einsum('bclgn,bcsgn->bclsg', cm, bm)
    y_diag = jnp.einsum('bclsgk,bcsgkp->bclgkp', cb[..., None] * decay, xdt)
    to_end = jnp.exp(cum[:, :, -1:] - cum)
    states = jnp.einsum('bclgn,bclgkp->bcgkpn', bm, xdt * to_end[..., None])
    chunk_decay = jnp.exp(cum[:, :, -1])

    def step(hs, inp):
        s_c, d_c = inp
        return hs * d_c[..., None, None] + s_c, hs

    h_init = h0.astype(f32).reshape(b, g, hg, hp, n)
    h_last, h_prev = lax.scan(step, h_init, (states.swapaxes(0, 1), chunk_decay.swapaxes(0, 1)))
    h_prev = h_prev.swapaxes(0, 1)
    y_off = jnp.einsum('bclgn,bcgkpn->bclgkp', cm, h_prev) * jnp.exp(cum)[..., None]
    y = (y_diag + y_off).reshape(b, t, nh, hp)
    return y, h_last.reshape(b, nh, hp, n)


def mamba_mixer(h, conv_hist, h0, chunk, p):
    w_in, conv_w, conv_b, dt_bias, a_log, d_skip, g_norm, w_out = p
    b, t, _ = h.shape
    gn = SSM_GROUPS * SSM_STATE
    proj = h @ w_in
    z = proj[..., :D_INNER]
    xbc = proj[..., D_INNER:D_INNER + CONV_DIM]
    dt_raw = proj[..., D_INNER + CONV_DIM:]
    xbc, new_hist = causal_dwconv(xbc, conv_hist, conv_w, conv_b)
    xbc = jax.nn.silu(xbc)
    xs = xbc[..., :D_INNER].reshape(b, t, SSM_HEADS, SSM_HEAD_DIM)
    bm = xbc[..., D_INNER:D_INNER + gn].reshape(b, t, SSM_GROUPS, SSM_STATE)
    cm = xbc[..., D_INNER + gn:].reshape(b, t, SSM_GROUPS, SSM_STATE)
    dt = jax.nn.softplus(dt_raw.astype(jnp.float32) + dt_bias.astype(jnp.float32))
    a = -jnp.exp(a_log.astype(jnp.float32))
    y, h_last = ssd(xs, dt, a, bm, cm, h0, chunk)
    y = y + xs.astype(jnp.float32) * d_skip.astype(jnp.float32)[:, None]
    y = y.reshape(b, t, D_INNER) * jax.nn.silu(z.astype(jnp.float32))
    yg = y.reshape(b, t, SSM_GROUPS, D_INNER // SSM_GROUPS)
    yg = yg * lax.rsqrt(jnp.mean(yg * yg, axis=-1, keepdims=True) + EPS)
    y = (yg.reshape(b, t, D_INNER) * g_norm.astype(jnp.float32)).astype(h.dtype)
    return y @ w_out, new_hist, h_last.astype(h.dtype)


def conv_ffn(h, hist, w_up, conv_w, conv_b, w_down):
    u = h @ w_up
    u, new_hist = causal_dwconv(u, hist, conv_w, conv_b)
    return (jax.nn.silu(u[..., :D_FF]) * u[..., D_FF:]) @ w_down, new_hist


def setup_inputs(seed: int = 0) -> dict:
    key = jax.random.key(seed)
    keys = iter(jax.random.split(key, 48))
    f32 = jnp.float32

    def nrm(shape, scale):
        return jax.random.normal(next(keys), shape, f32) * scale

    def gain(shape):
        return 1.0 + nrm(shape, 0.02)

    n_pages = PAST_LEN // PAGE_SIZE
    n_phys = (5 * DEC_BATCH * n_pages) // 4
    perm = jax.random.permutation(next(keys), n_phys)
    page_table = perm[:DEC_BATCH * n_pages].reshape(DEC_BATCH, n_pages).astype(jnp.int32)
    dt0 = jnp.exp(jax.random.uniform(next(keys), (N_SSM, SSM_HEADS), f32, math.log(1e-3), math.log(1e-1)))
    dt_bias = dt0 + jnp.log(-jnp.expm1(-dt0))
    a_log = jnp.log(jax.random.uniform(next(keys), (N_SSM, SSM_HEADS), f32, 1.0, 16.0))
    kscale = jax.random.uniform(next(keys), (N_MLA, n_phys, PAGE_SIZE, N_HEADS), f32, 0.5, 1.5)
    return {
        'x_prompt': nrm((BATCH, SEQ, D_MODEL), 1.0),
        'x_sample': nrm((DEC_BATCH, DEC_SEQ, D_MODEL), 1.0),
        'cache_mla_latent': nrm((N_MLA, n_phys, PAGE_SIZE, KV_LORA), 1.0),
        'cache_mla_krope': nrm((N_MLA, n_phys, PAGE_SIZE, ROPE_DIM), 1.0),
        'cache_mla_kscale': kscale,
        'page_table': page_table,
        'state_ssm': nrm((N_SSM, DEC_BATCH, SSM_HEADS, SSM_HEAD_DIM, SSM_STATE), 0.1),
        'state_ssm_conv': nrm((N_SSM, DEC_BATCH, SSM_CONV - 1, CONV_DIM), 1.0),
        'state_ffn_conv': nrm((DEPTH, DEC_BATCH, FFN_CONV - 1, 2 * D_FF), 1.0),
        'norm_mix': gain((DEPTH, D_MODEL)),
        'norm_ffn': gain((DEPTH, D_MODEL)),
        'w_mla_in': nrm((N_MLA, D_MODEL, Q_LORA + KV_LORA + ROPE_DIM), D_MODEL ** -0.5),
        'g_q_lora': gain((N_MLA, Q_LORA)),
        'g_kv_lora': gain((N_MLA, KV_LORA)),
        'w_uq': nrm((N_MLA, Q_LORA, N_HEADS * QK_DIM), Q_LORA ** -0.5),
        'w_uk': nrm((N_MLA, KV_LORA, N_HEADS, NOPE_DIM), KV_LORA ** -0.5),
        'w_uv': nrm((N_MLA, KV_LORA, N_HEADS, V_DIM), KV_LORA ** -0.5),
        'w_mla_out': nrm((N_MLA, N_HEADS * V_DIM, D_MODEL), (N_HEADS * V_DIM) ** -0.5),
        'g_q_nope': gain((N_MLA, NOPE_DIM)),
        'g_q_rope': gain((N_MLA, ROPE_DIM)),
        'g_k_nope': gain((N_MLA, NOPE_DIM)),
        'g_k_rope': gain((N_MLA, ROPE_DIM)),
        'w_ssm_in': nrm((N_SSM, D_MODEL, SSM_IN_DIM), D_MODEL ** -0.5),
        'conv_w_ssm': nrm((N_SSM, SSM_CONV, CONV_DIM), SSM_CONV ** -0.5),
        'conv_b_ssm': nrm((N_SSM, CONV_DIM), 0.01),
        'dt_bias': dt_bias,
        'a_log': a_log,
        'd_skip': 1.0 + nrm((N_SSM, SSM_HEADS), 0.1),
        'g_ssm_norm': gain((N_SSM, D_INNER)),
        'w_ssm_out': nrm((N_SSM, D_INNER, D_MODEL), D_INNER ** -0.5),
        'w_ffn_up': nrm((DEPTH, D_MODEL, 2 * D_FF), D_MODEL ** -0.5),
        'conv_w_ffn': nrm((DEPTH, FFN_CONV, 2 * D_FF), FFN_CONV ** -0.5),
        'conv_b_ffn': nrm((DEPTH, 2 * D_FF), 0.01),
        'w_ffn_down': nrm((DEPTH, D_FF, D_MODEL), D_FF ** -0.5),
    }


def reference(x_prompt, x_sample, cache_mla_latent, cache_mla_krope, cache_mla_kscale, page_table,
              state_ssm, state_ssm_conv, state_ffn_conv, norm_mix, norm_ffn,
              w_mla_in, g_q_lora, g_kv_lora, w_uq, w_uk, w_uv, w_mla_out,
              g_q_nope, g_q_rope, g_k_nope, g_k_rope,
              w_ssm_in, conv_w_ssm, conv_b_ssm, dt_bias, a_log, d_skip, g_ssm_norm, w_ssm_out,
              w_ffn_up, conv_w_ffn, conv_b_ffn, w_ffn_down):
    bp, sp, _ = x_prompt.shape
    bs, ss, _ = x_sample.shape
    past_len = page_table.shape[1] * PAGE_SIZE
    pos_p = jnp.arange(sp)
    pos_s = past_len + jnp.arange(ss)
    yp, ys = x_prompt, x_sample
    lat_p, kr_p, ksc_p, lat_s, kr_s, ksc_s = [], [], [], [], [], []
    ssm_p, sconv_p, ssm_s, sconv_s = [], [], [], []
    fconv_p, fconv_s = [], []
    for i in range(DEPTH):
        j = i // 2
        hp = rms_norm(yp, norm_mix[i])
        hs = rms_norm(ys, norm_mix[i])
        if i % 2 == 0:
            mw = (w_mla_in[j], g_q_lora[j], g_kv_lora[j], w_uq[j], w_uk[j], w_uv[j], w_mla_out[j],
                  g_q_nope[j], g_q_rope[j], g_k_nope[j], g_k_rope[j])
            op, c, kr, ksc = mla_prompt(hp, pos_p, mw)
            lat_p.append(c)
            kr_p.append(kr)
            ksc_p.append(ksc)
            os_, c, kr, ksc = mla_sample(hs, pos_s, cache_mla_latent, cache_mla_krope, cache_mla_kscale,
                                         j, page_table, mw)
            lat_s.append(c)
            kr_s.append(kr)
            ksc_s.append(ksc)
        else:
            sw = (w_ssm_in[j], conv_w_ssm[j], conv_b_ssm[j], dt_bias[j], a_log[j], d_skip[j],
                  g_ssm_norm[j], w_ssm_out[j])
            op, cv, st = mamba_mixer(hp, jnp.zeros((bp, SSM_CONV - 1, CONV_DIM), hp.dtype),
                                     jnp.zeros((bp, SSM_HEADS, SSM_HEAD_DIM, SSM_STATE), jnp.float32),
                                     min(SSM_CHUNK, sp), sw)
            sconv_p.append(cv)
            ssm_p.append(st)
            os_, cv, st = mamba_mixer(hs, state_ssm_conv[j], state_ssm[j], ss, sw)
            sconv_s.append(cv)
            ssm_s.append(st)
        yp = yp + op
        ys = ys + os_
        hp = rms_norm(yp, norm_ffn[i])
        hs = rms_norm(ys, norm_ffn[i])
        fp, cvp = conv_ffn(hp, jnp.zeros((bp, FFN_CONV - 1, 2 * D_FF), hp.dtype),
                           w_ffn_up[i], conv_w_ffn[i], conv_b_ffn[i], w_ffn_down[i])
        fs, cvs = conv_ffn(hs, state_ffn_conv[i], w_ffn_up[i], conv_w_ffn[i], conv_b_ffn[i], w_ffn_down[i])
        fconv_p.append(cvp)
        fconv_s.append(cvs)
        yp = yp + fp
        ys = ys + fs
    return (yp, ys,
            jnp.stack(lat_p), jnp.stack(kr_p), jnp.stack(ksc_p),
            jnp.stack(ssm_p), jnp.stack(sconv_p), jnp.stack(fconv_p),
            jnp.stack(lat_s), jnp.stack(kr_s), jnp.stack(ksc_s),
            jnp.stack(ssm_s), jnp.stack(sconv_s), jnp.stack(fconv_s))
```

```python
import functools
import math

import jax
import jax.numpy as jnp
from jax import lax
from jax.experimental import pallas as pl
from jax.experimental.pallas import tpu as pltpu

F32, BF16 = jnp.float32, jnp.bfloat16
EPS = 1e-6
ROPE_THETA = 10000.0
SSM_CHUNK = 128
LANE = 128
SUB_BF16 = 16
VMEM_LIMIT = 56 * 1024 * 1024
MM_TK_MAX = 2048
SAMPLE_ROWS_PAD = 16


def _cparams(*sem):
    return pltpu.CompilerParams(dimension_semantics=sem, vmem_limit_bytes=VMEM_LIMIT)


def _pick(n, cands):
    for c in cands:
        if n % c == 0:
            return c
    return n


def _nt(a, b):
    return lax.dot_general(a, b, (((1,), (1,)), ((), ())), preferred_element_type=F32)


def _dot(a, b):
    return jnp.dot(a, b, preferred_element_type=F32)


def _dot_f32(a, b):
    return jnp.dot(a, b, precision=lax.Precision.HIGHEST, preferred_element_type=F32)


def _split3(x):
    hi = x.astype(BF16)
    r1 = x - hi.astype(F32)
    mid = r1.astype(BF16)
    lo = (r1 - mid.astype(F32)).astype(BF16)
    return hi, mid, lo


def _rms(x, g, n):
    ms = jnp.sum(x * x, axis=-1, keepdims=True) * (1.0 / n)
    return (x * lax.rsqrt(ms + EPS)) * g


def _softplus(x):
    return jnp.maximum(x, 0.0) + jnp.log1p(jnp.exp(-jnp.abs(x)))


def _silu(x):
    return x * (1.0 / (1.0 + jnp.exp(-x)))


def _rope128(r, c, s1, s2, half):
    return r * c + pltpu.roll(r, LANE - half, 1) * s1 + pltpu.roll(r, half, 1) * s2


def _rmsnorm_kernel(x_ref, g_ref, o_ref):
    x = x_ref[...]
    o_ref[...] = _rms(x, g_ref[...], x.shape[-1]).astype(o_ref.dtype)


def rmsnorm(x, g, out_dtype=BF16):
    m, d = x.shape
    tm = _pick(m, (256, 128, 64, 32, 16))
    return pl.pallas_call(
        _rmsnorm_kernel,
        grid=(m // tm,),
        in_specs=[pl.BlockSpec((tm, d), lambda i: (i, 0)), pl.BlockSpec((1, d), lambda i: (0, 0))],
        out_specs=pl.BlockSpec((tm, d), lambda i: (i, 0)),
        out_shape=jax.ShapeDtypeStruct((m, d), out_dtype),
        compiler_params=_cparams("parallel"),
        name="rmsnorm",
    )(x, g.reshape(1, d))


def _mm_kernel(*refs, nk, k_rem, has_res):
    if has_res:
        a_ref, w_ref, r_ref, o_ref = refs[:4]
    else:
        a_ref, w_ref, o_ref = refs[:3]
        r_ref = None
    acc_ref = refs[-1] if nk > 1 else None
    k = pl.program_id(2)
    a = a_ref[...]
    w = w_ref[...]
    if k_rem:
        limit = jnp.where(k == nk - 1, k_rem, a.shape[1])
        a = jnp.where(lax.broadcasted_iota(jnp.int32, a.shape, 1) < limit, a, jnp.zeros_like(a))
        w = jnp.where(lax.broadcasted_iota(jnp.int32, w.shape, 0) < limit, w, jnp.zeros_like(w))
    p = _dot(a.astype(BF16), w.astype(BF16))

    def finish(r):
        if has_res:
            r = r + r_ref[...]
        o_ref[...] = r.astype(o_ref.dtype)

    if nk == 1:
        finish(p)
        return

    @pl.when(k == 0)
    def _():
        acc_ref[...] = p

    @pl.when(k > 0)
    def _():
        acc_ref[...] += p

    @pl.when(k == nk - 1)
    def _():
        finish(acc_ref[...])


def matmul(a, w, res=None, out_dtype=F32, tm=None, tn=None, tk=None, name="matmul"):
    m, k = a.shape
    k2, n = w.shape
    assert k == k2
    tm = tm or _pick(m, (1088, 1024, 512, 256, 128, 64, 32, 16))
    tn = tn or _pick(n, (512, 640, 384, 256, 128))
    if tk is None:
        tk = k if k <= MM_TK_MAX else (MM_TK_MAX if k % MM_TK_MAX == 0 else MM_TK_MAX // 2)
    nk = pl.cdiv(k, tk)
    k_rem = k % tk
    in_specs = [pl.BlockSpec((tm, tk), lambda i, j, kk: (i, kk)),
                pl.BlockSpec((tk, tn), lambda i, j, kk: (kk, j))]
    args = [a, w]
    if res is not None:
        in_specs.append(pl.BlockSpec((tm, tn), lambda i, j, kk: (i, j)))
        args.append(res)
    return pl.pallas_call(
        functools.partial(_mm_kernel, nk=nk, k_rem=k_rem, has_res=res is not None),
        grid=(m // tm, n // tn, nk),
        in_specs=in_specs,
        out_specs=pl.BlockSpec((tm, tn), lambda i, j, kk: (i, j)),
        out_shape=jax.ShapeDtypeStruct((m, n), out_dtype),
        scratch_shapes=[pltpu.VMEM((tm, tn), F32)] if nk > 1 else [],
        compiler_params=_cparams("parallel", "parallel", "arbitrary"),
        name=name,
    )(*args)


def _mla_post_a_kernel(a_ref, gq_ref, gkv_ref, gkr_ref, c_ref, s1_ref, s2_ref,
                       qa_ref, lat_ref, kr_ref, *, ql, kvl, rope):
    a = a_ref[...]
    qa_ref[...] = _rms(a[:, :ql], gq_ref[...], ql).astype(qa_ref.dtype)
    lat_ref[...] = _rms(a[:, ql:ql + kvl], gkv_ref[...], kvl)
    r = _rms(a[:, ql + kvl:], gkr_ref[...], rope)
    kr_ref[...] = _rope128(r, c_ref[...], s1_ref[...], s2_ref[...], rope // 2)


def mla_post_a(a, g_qa, g_kva, g_kr_pad, tabs, ql, kvl, rope):
    t = a.shape[0]
    tm = _pick(t, (256, 128, 64, 32, 16))
    row = lambda w: pl.BlockSpec((tm, w), lambda i: (i, 0))
    vec = lambda w: pl.BlockSpec((1, w), lambda i: (0, 0))
    return pl.pallas_call(
        functools.partial(_mla_post_a_kernel, ql=ql, kvl=kvl, rope=rope),
        grid=(t // tm,),
        in_specs=[row(a.shape[1]), vec(ql), vec(kvl), vec(LANE), row(LANE), row(LANE), row(LANE)],
        out_specs=[row(ql), row(kvl), row(LANE)],
        out_shape=[jax.ShapeDtypeStruct((t, ql), BF16), jax.ShapeDtypeStruct((t, kvl), F32),
                   jax.ShapeDtypeStruct((t, LANE), F32)],
        compiler_params=_cparams("parallel"),
        name="mla_post_a",
    )(a, g_qa.reshape(1, ql), g_kva.reshape(1, kvl), g_kr_pad, *tabs)


def _q_post_kernel(q_ref, gn_ref, gr_ref, c_ref, s1_ref, s2_ref, o_ref, *, hb, nope, rope):
    c, s1, s2 = c_ref[...], s1_ref[...], s2_ref[...]
    for h in range(hb):
        lo = h * 2 * LANE
        qn = _rms(q_ref[:, lo:lo + LANE], gn_ref[...], nope)
        o_ref[:, lo:lo + LANE] = qn.astype(o_ref.dtype)
        r = _rms(q_ref[:, lo + LANE:lo + 2 * LANE], gr_ref[...], rope)
        o_ref[:, lo + LANE:lo + 2 * LANE] = _rope128(r, c, s1, s2, rope // 2).astype(o_ref.dtype)


def q_post(q_raw, g_qn, g_qr_pad, tabs, nope, rope, out_dtype):
    t, w = q_raw.shape
    tm = _pick(t, (256, 128, 64, 32, 16))
    hb = _pick(w // (2 * LANE), (4, 2, 1))
    bw = hb * 2 * LANE
    tab = pl.BlockSpec((tm, LANE), lambda i, j: (i, 0))
    vec = pl.BlockSpec((1, LANE), lambda i, j: (0, 0))
    return pl.pallas_call(
        functools.partial(_q_post_kernel, hb=hb, nope=nope, rope=rope),
        grid=(t // tm, w // bw),
        in_specs=[pl.BlockSpec((tm, bw), lambda i, j: (i, j)), vec, vec, tab, tab, tab],
        out_specs=pl.BlockSpec((tm, bw), lambda i, j: (i, j)),
        out_shape=jax.ShapeDtypeStruct((t, w), out_dtype),
        compiler_params=_cparams("parallel", "parallel"),
        name="q_post",
    )(q_raw, g_qn.reshape(1, LANE), g_qr_pad, *tabs)


def _kv_up_kernel(c_ref, kr_ref, wuk_ref, wuv_ref, gkn_ref, kcat_ref, v_ref, ks_ref, *, nh, nope):
    c = c_ref[...].astype(BF16)
    krb = kr_ref[...].astype(BF16)
    lane = lax.broadcasted_iota(jnp.int32, ks_ref.shape, 1)
    ks_out = jnp.zeros(ks_ref.shape, F32)
    for h in range(nh):
        kn = _dot(c, wuk_ref[:, h * LANE:(h + 1) * LANE])
        ksc = lax.rsqrt(jnp.sum(kn * kn, axis=-1, keepdims=True) * (1.0 / nope) + EPS)
        ks_out = jnp.where(lane == h, ksc, ks_out)
        kcat_ref[:, 2 * h * LANE:(2 * h + 1) * LANE] = ((kn * ksc) * gkn_ref[...]).astype(BF16)
        kcat_ref[:, (2 * h + 1) * LANE:(2 * h + 2) * LANE] = krb
        v_ref[:, h * LANE:(h + 1) * LANE] = _dot(c, wuv_ref[:, h * LANE:(h + 1) * LANE]).astype(BF16)
    ks_ref[...] = ks_out


def kv_up(c, kr_pad, wuk_bf, wuv_bf, g_kn, nh, nope):
    t, kvl = c.shape
    tm = _pick(t, (256, 128, 64, 32, 16))
    row = lambda w: pl.BlockSpec((tm, w), lambda i: (i, 0))
    full = lambda r, w: pl.BlockSpec((r, w), lambda i: (0, 0))
    return pl.pallas_call(
        functools.partial(_kv_up_kernel, nh=nh, nope=nope),
        grid=(t // tm,),
        in_specs=[row(kvl), row(LANE), full(kvl, nh * LANE), full(kvl, nh * LANE), full(1, LANE)],
        out_specs=[row(2 * nh * LANE), row(nh * LANE), row(LANE)],
        out_shape=[jax.ShapeDtypeStruct((t, 2 * nh * LANE), BF16), jax.ShapeDtypeStruct((t, nh * LANE), BF16),
                   jax.ShapeDtypeStruct((t, LANE), F32)],
        compiler_params=_cparams("parallel"),
        name="kv_up",
    )(c, kr_pad, wuk_bf, wuv_bf, g_kn.reshape(1, LANE))


def _flash_kernel(q_ref, k_ref, v_ref, o_ref, *, tq, scale):
    i = pl.program_id(2)
    q = q_ref[...]
    rowp = i * tq + lax.broadcasted_iota(jnp.int32, (tq, tq), 0)
    colp = lax.broadcasted_iota(jnp.int32, (tq, tq), 1)

    def body(j, carry):
        m, l, acc = carry
        off = pl.multiple_of(j * tq, tq)
        s = _nt(q, k_ref[pl.ds(off, tq), :]) * scale
        s = jnp.where(rowp >= colp + j * tq, s, -jnp.inf)
        m_new = jnp.maximum(m, jnp.max(s, axis=-1, keepdims=True))
        alpha = jnp.exp(m - m_new)
        p = jnp.exp(s - m_new)
        l = l * alpha + jnp.sum(p, axis=-1, keepdims=True)
        acc = acc * alpha + _dot(p.astype(BF16), v_ref[pl.ds(off, tq), :])
        return m_new, l, acc

    init = (jnp.full((tq, 1), -jnp.inf, F32), jnp.zeros((tq, 1), F32), jnp.zeros((tq, LANE), F32))
    _, l, acc = lax.fori_loop(0, i + 1, body, init)
    o_ref[...] = (acc / l).astype(o_ref.dtype)


def flash_prompt(qcat, kcat, v, bp, sp, nh, scale):
    tq = _pick(sp, (256, 128))
    nq = sp // tq
    return pl.pallas_call(
        functools.partial(_flash_kernel, tq=tq, scale=scale),
        grid=(bp, nh, nq),
        in_specs=[pl.BlockSpec((tq, 2 * LANE), lambda b, h, i: (b * nq + i, h)),
                  pl.BlockSpec((sp, 2 * LANE), lambda b, h, i: (b, h)),
                  pl.BlockSpec((sp, LANE), lambda b, h, i: (b, h))],
        out_specs=pl.BlockSpec((tq, LANE), lambda b, h, i: (b * nq + i, h)),
        out_shape=jax.ShapeDtypeStruct((bp * sp, nh * LANE), BF16),
        compiler_params=_cparams("parallel", "parallel", "arbitrary"),
        name="flash_prompt",
    )(qcat, kcat, v)


def _q_absorb_kernel(q_ref, w_ref, g_ref, o_ref):
    qn = (q_ref[:, :LANE] * g_ref[...]).astype(BF16)
    o_ref[...] = _nt(qn, w_ref[...]).astype(o_ref.dtype)


def q_absorb(qs, wuk_bf, g_kn, nh, kvl):
    ts = qs.shape[0]
    return pl.pallas_call(
        _q_absorb_kernel,
        grid=(nh,),
        in_specs=[pl.BlockSpec((ts, 2 * LANE), lambda h: (0, h)),
                  pl.BlockSpec((kvl, LANE), lambda h: (0, h)),
                  pl.BlockSpec((1, LANE), lambda h: (0, 0))],
        out_specs=pl.BlockSpec((ts, kvl), lambda h: (0, h)),
        out_shape=jax.ShapeDtypeStruct((ts, nh * kvl), BF16),
        compiler_params=_cparams("parallel"),
        name="q_absorb",
    )(qs, wuk_bf, g_kn.reshape(1, LANE))


def _o_up_kernel(o_ref, w_ref, out_ref):
    out_ref[...] = _dot(o_ref[...], w_ref[...]).astype(out_ref.dtype)


def o_up(o_lat, wuv_bf, nh, kvl):
    ts = o_lat.shape[0]
    return pl.pallas_call(
        _o_up_kernel,
        grid=(nh,),
        in_specs=[pl.BlockSpec((ts, kvl), lambda h: (0, h)), pl.BlockSpec((kvl, LANE), lambda h: (0, h))],
        out_specs=pl.BlockSpec((ts, LANE), lambda h: (0, h)),
        out_shape=jax.ShapeDtypeStruct((ts, nh * LANE), BF16),
        compiler_params=_cparams("parallel"),
        name="o_up",
    )(o_lat, wuv_bf)


def _decode_kernel(pt_ref, qlat_ref, qr_ref, slat_ref, skr_ref, sks_ref, *rest, pp, nh, tq, scale):
    pages = rest[:3 * pp]
    o_ref = rest[3 * pp]
    m_ref, l_ref, acc_ref = rest[3 * pp + 1:]
    c = pl.program_id(1)
    qlat = qlat_ref[0]
    qr = qr_ref[0]
    rows = tq * nh
    rr = lax.broadcasted_iota(jnp.int32, (rows, nh), 0)
    hh = lax.broadcasted_iota(jnp.int32, (rows, nh), 1)
    sel = rr == hh
    for t in range(1, tq):
        sel = sel | (rr == hh + t * nh)
    head_of_row = sel.astype(BF16)

    def scores(lat_b, kr, ks):
        s_n = _nt(qlat, lat_b)
        s_r = _nt(qr, kr.astype(BF16))
        hi, mid, lo = _split3(ks)
        ksx = _nt(head_of_row, hi) + _nt(head_of_row, mid) + _nt(head_of_row, lo)
        return (s_n * ksx + s_r) * scale

    @pl.when(c == 0)
    def _():
        lat_b = slat_ref[0].astype(BF16)
        s = scores(lat_b, skr_ref[0], sks_ref[0])
        u = lax.broadcasted_iota(jnp.int32, s.shape, 1)
        r = lax.broadcasted_iota(jnp.int32, s.shape, 0)
        s = jnp.where(u * nh <= r, s, -jnp.inf)
        m0 = jnp.max(s, axis=-1, keepdims=True)
        p = jnp.exp(s - m0)
        m_ref[...] = m0
        l_ref[...] = jnp.sum(p, axis=-1, keepdims=True)
        acc_ref[...] = _dot(p.astype(BF16), lat_b)

    lats = [pages[3 * s][0, 0].astype(BF16) for s in range(pp)]
    ss = [scores(lats[s], pages[3 * s + 1][0, 0], pages[3 * s + 2][0, 0]) for s in range(pp)]
    m_prev = m_ref[...]
    m_new = m_prev
    for s in ss:
        m_new = jnp.maximum(m_new, jnp.max(s, axis=-1, keepdims=True))
    alpha = jnp.exp(m_prev - m_new)
    l = l_ref[...] * alpha
    acc = acc_ref[...] * alpha
    for s, lat_b in zip(ss, lats):
        p = jnp.exp(s - m_new)
        l = l + jnp.sum(p, axis=-1, keepdims=True)
        acc = acc + _dot(p.astype(BF16), lat_b)
    m_ref[...] = m_new
    l_ref[...] = l
    acc_ref[...] = acc

    @pl.when(c == pl.num_programs(1) - 1)
    def _():
        o_ref[0] = (acc / l).astype(o_ref.dtype)


def decode_attn(qlat, qr, self_lat, self_kr, self_ks, cache_lat, cache_kr, cache_ks, page_table, layer, nh, tq, scale):
    bs, rows, kvl = qlat.shape
    n_pages = page_table.shape[1]
    page = cache_lat.shape[2]
    rope = cache_kr.shape[3]
    pp = _pick(n_pages, (8, 4, 2, 1))
    nc = n_pages // pp

    def per_b(shape):
        return pl.BlockSpec((1,) + shape, lambda b, c, pt: (b, 0, 0))

    def paged(width, s):
        return pl.BlockSpec((1, 1, page, width), lambda b, c, pt: (layer, pt[b * n_pages + c * pp + s], 0, 0))

    in_specs = [per_b((rows, kvl)), per_b((rows, rope)), per_b((page, kvl)), per_b((page, rope)), per_b((page, nh))]
    args = [qlat, qr, self_lat, self_kr, self_ks]
    for s in range(pp):
        in_specs += [paged(kvl, s), paged(rope, s), paged(nh, s)]
        args += [cache_lat, cache_kr, cache_ks]
    return pl.pallas_call(
        functools.partial(_decode_kernel, pp=pp, nh=nh, tq=tq, scale=scale),
        grid_spec=pltpu.PrefetchScalarGridSpec(
            num_scalar_prefetch=1,
            grid=(bs, nc),
            in_specs=in_specs,
            out_specs=pl.BlockSpec((1, rows, kvl), lambda b, c, pt: (b, 0, 0)),
            scratch_shapes=[pltpu.VMEM((rows, 1), F32), pltpu.VMEM((rows, 1), F32), pltpu.VMEM((rows, kvl), F32)],
        ),
        out_shape=jax.ShapeDtypeStruct((bs, rows, kvl), BF16),
        compiler_params=_cparams("parallel", "arbitrary"),
        name="decode_attn",
    )(page_table.reshape(-1), *args)


def _conv_rows(x, hist, w, bias, kf):
    row = lax.broadcasted_iota(jnp.int32, x.shape, 0)
    y = bias + x * w[kf - 1:kf]
    for s in range(1, kf):
        xs = pltpu.roll(x, s, 0)
        for t in range(s):
            xs = jnp.where(row == t, hist[kf - 1 - s + t:kf - s + t], xs)
        y = y + xs * w[kf - 1 - s:kf - s]
    return y


def _conv_prompt_kernel(*refs, kf, gate):
    if gate:
        ug, uv, hg, hv, wg, wv, bg, bv, o_ref = refs
        a = _conv_rows(ug[...], hg[0], wg[...], bg[...], kf)
        b = _conv_rows(uv[...], hv[0], wv[...], bv[...], kf)
        o_ref[...] = (_silu(a) * b).astype(o_ref.dtype)
    else:
        u, h, w, b, o_ref = refs
        o_ref[...] = _silu(_conv_rows(u[...], h[0], w[...], b[...], kf)).astype(o_ref.dtype)


def conv_prompt(u, hist, w, bias, nb, sl, col0, width, gate, out_dtype):
    kf = w.shape[0]
    tc = _pick(math.gcd(width, col0) if col0 else width, (256, 128))
    nj = width // tc
    o0 = col0 // tc
    halves = (0, nj) if gate else (0,)
    u_specs = [pl.BlockSpec((sl, tc), lambda b, j, o=o: (b, o0 + o + j)) for o in halves]
    h_specs = [pl.BlockSpec((1, kf - 1, tc), lambda b, j, o=o: (b, 0, o + j)) for o in halves]
    w_specs = [pl.BlockSpec((kf, tc), lambda b, j, o=o: (0, o + j)) for o in halves]
    b_specs = [pl.BlockSpec((1, tc), lambda b, j, o=o: (0, o + j)) for o in halves]
    n_in = len(halves)
    return pl.pallas_call(
        functools.partial(_conv_prompt_kernel, kf=kf, gate=gate),
        grid=(nb, nj),
        in_specs=u_specs + h_specs + w_specs + b_specs,
        out_specs=pl.BlockSpec((sl, tc), lambda b, j: (b, j)),
        out_shape=jax.ShapeDtypeStruct((nb * sl, width), out_dtype),
        compiler_params=_cparams("parallel", "parallel"),
        name="conv_prompt_gate" if gate else "conv_prompt",
    )(*([u] * n_in + [hist] * n_in + [w] * n_in + [bias.reshape(1, -1)] * n_in))


def _conv_sample_kernel(*refs, kf, ss, gate):
    def conv(u_ref, h_ref, w_ref, b_ref, t):
        y = b_ref[...]
        for j in range(kf):
            src = h_ref[t + j] if t + j < kf - 1 else u_ref[t + j - (kf - 1)]
            y = y + src * w_ref[j:j + 1, :]
        return y

    if gate:
        ug, uv, hg, hv, wg, wv, bg, bv, o_ref = refs
        for t in range(ss):
            o_ref[t] = (_silu(conv(ug, hg, wg, bg, t)) * conv(uv, hv, wv, bv, t)).astype(o_ref.dtype)
    else:
        u, h, w, b, o_ref = refs
        for t in range(ss):
            o_ref[t] = _silu(conv(u, h, w, b, t)).astype(o_ref.dtype)


def conv_sample(u_t, hist_t, w, bias, width, gate, out_dtype):
    ss, bs, _ = u_t.shape
    kf = w.shape[0]
    tc = _pick(width, (256, 128))
    nj = width // tc
    halves = (0, nj) if gate else (0,)
    u_specs = [pl.BlockSpec((ss, bs, tc), lambda j, o=o: (0, 0, o + j)) for o in halves]
    h_specs = [pl.BlockSpec((kf - 1, bs, tc), lambda j, o=o: (0, 0, o + j)) for o in halves]
    w_specs = [pl.BlockSpec((kf, tc), lambda j, o=o: (0, o + j)) for o in halves]
    b_specs = [pl.BlockSpec((1, tc), lambda j, o=o: (0, o + j)) for o in halves]
    n_in = len(halves)
    return pl.pallas_call(
        functools.partial(_conv_sample_kernel, kf=kf, ss=ss, gate=gate),
        grid=(nj,),
        in_specs=u_specs + h_specs + w_specs + b_specs,
        out_specs=pl.BlockSpec((ss, bs, tc), lambda j: (0, 0, j)),
        out_shape=jax.ShapeDtypeStruct((ss, bs, width), out_dtype),
        compiler_params=_cparams("parallel"),
        name="conv_sample_gate" if gate else "conv_sample",
    )(*([u_t] * n_in + [hist_t] * n_in + [w] * n_in + [bias.reshape(1, -1)] * n_in))


def _gated_group_norm(y, z, gn):
    y = y * _silu(z)
    ms = jnp.sum(y * y, axis=-1, keepdims=True) * (1.0 / y.shape[-1])
    return (y * lax.rsqrt(ms + EPS)) * gn


def _ssd_prompt_kernel(xs_ref, bm_ref, cm_ref, z_ref, dt_ref, dtt_ref, bias_ref, alog_ref, dskip_ref, gn_ref,
                       biast_ref, alogt_ref, y_ref, hl_ref, h_ref, *, hp):
    c = pl.program_id(2)

    @pl.when(c == 0)
    def _():
        h_ref[...] = jnp.zeros_like(h_ref)

    xs = xs_ref[...]
    L, gw = xs.shape
    hpl = LANE // hp
    bm = bm_ref[...].astype(BF16)
    cm = cm_ref[...].astype(BF16)
    ri = lax.broadcasted_iota(jnp.int32, (L, L), 0)
    ci = lax.broadcasted_iota(jnp.int32, (L, L), 1)
    causal = ri >= ci
    dt = _softplus(dt_ref[...] + bias_ref[...])
    dta = dt * (-jnp.exp(alog_ref[...]))
    cum = _dot_f32(causal.astype(F32), dta)
    dtt = _softplus(dtt_ref[...] + biast_ref[...])
    cumt = _dot_f32(dtt * (-jnp.exp(alogt_ref[...])), (ri <= ci).astype(F32))
    xdt = xs * dt
    cb = _nt(cm, bm)
    cum_last = cum[L - 1:L, :]
    xw = xdt * jnp.exp(cum_last - cum)
    ecum = jnp.exp(cum)
    eye = (ri == ci).astype(BF16)
    lane_head = lax.broadcasted_iota(jnp.int32, (L, LANE), 1) // hp
    row_head = lax.broadcasted_iota(jnp.int32, (LANE, 1), 0) // hp
    ys = []
    for pi in range(gw // LANE):
        sl = slice(pi * LANE, (pi + 1) * LANE)
        xdt_p = xdt[:, sl]
        y_p = jnp.zeros((L, LANE), F32)
        dcol = jnp.zeros((LANE, 1), F32)
        for q in range(hpl):
            k = pi * hpl + q
            colb = jnp.broadcast_to(cum[:, k * hp:k * hp + 1], (L, L))
            rowb = jnp.broadcast_to(cumt[k:k + 1, :], (L, L))
            dec = jnp.where(causal, jnp.exp(colb - rowb), 0.0)
            xk = jnp.where(lane_head == q, xdt_p, 0.0).astype(BF16)
            y_p = y_p + _dot((cb * dec).astype(BF16), xk)
            dcol = jnp.where(row_head == q, jnp.exp(cumt[k:k + 1, L - 1:L]), dcol)
        h_prev = h_ref[sl, :]
        y_p = y_p + _nt(cm, h_prev.astype(BF16)) * ecum[:, sl]
        xwt = _nt(eye, xw[:, sl].astype(BF16))
        h_ref[sl, :] = h_prev * dcol + _dot(xwt.astype(BF16), bm)
        ys.append(y_p)
    y = jnp.concatenate(ys, axis=1) + xs * dskip_ref[...]
    y_ref[...] = _gated_group_norm(y, z_ref[...], gn_ref[...]).astype(y_ref.dtype)

    @pl.when(c == pl.num_programs(2) - 1)
    def _():
        hl_ref[0] = h_ref[...]


def ssd_prompt(xbc, proj, dt_rep, dt_t, bias_rep, alog_rep, dskip_rep, gnorm, bias_t, alog_t,
               bp, sp, nh, hp, ns, ng):
    L = min(SSM_CHUNK, sp)
    nch = sp // L
    d_inner = nh * hp
    gw = d_inner // ng
    hg = nh // ng
    b0 = d_inner // ns
    rowb = lambda w, f: pl.BlockSpec((L, w), f)
    vec = pl.BlockSpec((1, gw), lambda b, g, c: (0, g))
    vect = pl.BlockSpec((hg, 1), lambda b, g, c: (g, 0))
    return pl.pallas_call(
        functools.partial(_ssd_prompt_kernel, hp=hp),
        grid=(bp, ng, nch),
        in_specs=[rowb(gw, lambda b, g, c: (b * nch + c, g)),
                  rowb(ns, lambda b, g, c: (b * nch + c, b0 + g)),
                  rowb(ns, lambda b, g, c: (b * nch + c, b0 + ng + g)),
                  rowb(gw, lambda b, g, c: (b * nch + c, g)),
                  rowb(gw, lambda b, g, c: (b * nch + c, g)),
                  pl.BlockSpec((hg, L), lambda b, g, c: (g, b * nch + c)),
                  vec, vec, vec, vec, vect, vect],
        out_specs=[rowb(gw, lambda b, g, c: (b * nch + c, g)),
                   pl.BlockSpec((1, gw, ns), lambda b, g, c: (b, g, 0))],
        out_shape=[jax.ShapeDtypeStruct((bp * sp, d_inner), BF16),
                   jax.ShapeDtypeStruct((bp, d_inner, ns), F32)],
        scratch_shapes=[pltpu.VMEM((gw, ns), F32)],
        compiler_params=_cparams("parallel", "parallel", "arbitrary"),
        name="ssd_prompt",
    )(xbc, xbc, xbc, proj, dt_rep, dt_t, bias_rep, alog_rep, dskip_rep, gnorm, bias_t, alog_t)


def _ssd_sample_kernel(xs_ref, bm_ref, cm_ref, z_ref, dt_ref, bias_ref, alog_ref, dskip_ref, gn_ref, h0_ref,
                       y_ref, hn_ref, *, ss, hp):
    xs = xs_ref[0]
    lp, gw = xs.shape
    hpl = LANE // hp
    bm = bm_ref[0]
    cm = cm_ref[0]
    row = lax.broadcasted_iota(jnp.int32, (lp, gw), 0)
    dt = jnp.where(row < ss, _softplus(dt_ref[0] + bias_ref[...]), 0.0)
    dta = dt * (-jnp.exp(alog_ref[...]))
    cums = [dta[0:1]]
    for l in range(1, ss):
        cums.append(cums[-1] + dta[l:l + 1])
    xdt = xs * dt
    h0 = h0_ref[0]
    yoff = _nt(cm.astype(BF16), h0.astype(BF16))
    y = jnp.zeros((lp, gw), F32)
    cum_full = jnp.broadcast_to(cums[ss - 1], (lp, gw))
    for l in range(ss):
        acc = yoff[l:l + 1] * jnp.exp(cums[l]) + xs[l:l + 1] * dskip_ref[...]
        for s in range(l + 1):
            cb = jnp.sum(cm[l:l + 1] * bm[s:s + 1], axis=-1, keepdims=True)
            acc = acc + (cb * jnp.exp(cums[l] - cums[s])) * xdt[s:s + 1]
        y = jnp.where(row == l, acc, y)
        cum_full = jnp.where(row == l, cums[l], cum_full)
    y_ref[0] = _gated_group_norm(y, z_ref[0], gn_ref[...]).astype(y_ref.dtype)

    xw = xdt * jnp.exp(cums[ss - 1] - cum_full)
    xw_pad = jnp.concatenate([xw, jnp.zeros((LANE - lp, gw), F32)], axis=0).astype(BF16)
    bm_pad = jnp.concatenate([bm, jnp.zeros((LANE - lp, bm.shape[1]), F32)], axis=0).astype(BF16)
    ri = lax.broadcasted_iota(jnp.int32, (LANE, LANE), 0)
    ci = lax.broadcasted_iota(jnp.int32, (LANE, LANE), 1)
    eye = (ri == ci).astype(BF16)
    row_head = lax.broadcasted_iota(jnp.int32, (LANE, 1), 0) // hp
    dlast = jnp.exp(cums[ss - 1])
    for pi in range(gw // LANE):
        sl = slice(pi * LANE, (pi + 1) * LANE)
        dcol = jnp.zeros((LANE, 1), F32)
        for q in range(hpl):
            k = pi * hpl + q
            dcol = jnp.where(row_head == q, dlast[0:1, k * hp:k * hp + 1], dcol)
        xwt = _nt(eye, xw_pad[:, sl])
        hn_ref[0, sl, :] = h0[sl, :] * dcol + _dot(xwt.astype(BF16), bm_pad)


def ssd_sample(xs, bm, cm, z, dt_rep, bias_rep, alog_rep, dskip_rep, gnorm, h0, ss, hp, ng):
    bs, lp, d_inner = xs.shape
    ns = h0.shape[-1]
    gw = d_inner // ng
    act = lambda w: pl.BlockSpec((1, lp, w), lambda b, g: (b, 0, g))
    vec = pl.BlockSpec((1, gw), lambda b, g: (0, g))
    st = pl.BlockSpec((1, gw, ns), lambda b, g: (b, g, 0))
    return pl.pallas_call(
        functools.partial(_ssd_sample_kernel, ss=ss, hp=hp),
        grid=(bs, ng),
        in_specs=[act(gw), act(ns), act(ns), act(gw), act(gw), vec, vec, vec, vec, st],
        out_specs=[act(gw), st],
        out_shape=[jax.ShapeDtypeStruct((bs, lp, d_inner), BF16), jax.ShapeDtypeStruct(h0.shape, F32)],
        compiler_params=_cparams("parallel", "parallel"),
        name="ssd_sample",
    )(xs, bm, cm, z, dt_rep, bias_rep, alog_rep, dskip_rep, gnorm, h0)


def kernel(x_prompt, x_sample, cache_mla_latent, cache_mla_krope, cache_mla_kscale, page_table, state_ssm, state_ssm_conv, state_ffn_conv, norm_mix, norm_ffn, w_mla_in, g_q_lora, g_kv_lora, w_uq, w_uk, w_uv, w_mla_out, g_q_nope, g_q_rope, g_k_nope, g_k_rope, w_ssm_in, conv_w_ssm, conv_b_ssm, dt_bias, a_log, d_skip, g_ssm_norm, w_ssm_out, w_ffn_up, conv_w_ffn, conv_b_ffn, w_ffn_down):
    bp, sp, d = x_prompt.shape
    bs, ss, _ = x_sample.shape
    tp, ts = bp * sp, bs * ss
    depth = norm_mix.shape[0]
    kvl, nh, nope = w_uk.shape[1:]
    vd = w_uv.shape[3]
    ql = g_q_lora.shape[1]
    rope = g_q_rope.shape[1]
    qk = nope + rope
    assert nope == LANE and vd == LANE and 2 * rope == LANE
    n_pages = page_table.shape[1]
    page = cache_mla_latent.shape[2]
    past_len = n_pages * page
    scale = 1.0 / math.sqrt(qk)
    nsh, hp, ns = state_ssm.shape[2:]
    d_inner = nsh * hp
    conv_dim = conv_w_ssm.shape[2]
    ng = (conv_dim - d_inner) // (2 * ns)
    kc = conv_w_ssm.shape[1]
    dff = w_ffn_down.shape[1]
    kf = conv_w_ffn.shape[1]
    assert ss >= kc - 1 and ss >= kf - 1 and ss <= SAMPLE_ROWS_PAD and LANE % hp == 0 and ns == LANE
    lp = SAMPLE_ROWS_PAD

    half = rope // 2
    inv = jnp.exp(-math.log(ROPE_THETA) * jnp.arange(half, dtype=F32) / half)
    pos = jnp.concatenate([jnp.tile(jnp.arange(sp), bp), jnp.tile(past_len + jnp.arange(ss), bs)])
    ang = pos.astype(F32)[:, None] * inv[None, :]
    cos, sin = jnp.cos(ang), jnp.sin(ang)
    zt = jnp.zeros_like(cos)
    tabs = (jnp.concatenate([cos, cos, zt, zt], 1), jnp.concatenate([-sin, zt, zt, zt], 1),
            jnp.concatenate([zt, sin, zt, zt], 1))
    tabs_s = tuple(t[tp:] for t in tabs)

    def pad_lane(g):
        return jnp.pad(g, (0, LANE - g.shape[0])).reshape(1, LANE)

    x = jnp.concatenate([x_prompt.reshape(tp, d), x_sample.reshape(ts, d)], axis=0)
    lat_o, kr_o, ksc_o, ssm_o, sconv_o, fconv_o = [], [], [], [], [], []

    for i in range(depth):
        j = i // 2
        h = rmsnorm(x, norm_mix[i])
        if i % 2 == 0:
            w_in = jnp.pad(w_mla_in[j], ((0, 0), (0, LANE - rope)))
            a = matmul(h, w_in, tm=_pick(tp + ts, (544, 512, 256, 128, 64, 32, 16)), tn=w_in.shape[1],
                       tk=min(d, 1024), name="mla_in")
            qa, c, krp = mla_post_a(a, g_q_lora[j], g_kv_lora[j], pad_lane(g_k_rope[j]), tabs, ql, kvl, rope)
            w_uq_pad = jnp.pad(w_uq[j].reshape(ql, nh, qk), ((0, 0), (0, 0), (0, 2 * LANE - qk))).reshape(ql, nh * 2 * LANE)
            q_raw = matmul(qa, w_uq_pad, name="mla_uq")
            g_qr_pad = pad_lane(g_q_rope[j])
            qcat = q_post(q_raw, g_q_nope[j], g_qr_pad, tabs, nope, rope, BF16)
            wuk_bf = w_uk[j].reshape(kvl, nh * nope).astype(BF16)
            wuv_bf = w_uv[j].reshape(kvl, nh * vd).astype(BF16)
            kcat, v, ksc = kv_up(c, krp, wuk_bf, wuv_bf, g_k_nope[j], nh, nope)
            o_p = flash_prompt(qcat, kcat, v, bp, sp, nh, scale)
            qs = q_post(q_raw[tp:], g_q_nope[j], g_qr_pad, tabs_s, nope, rope, F32)
            qlat = q_absorb(qs, wuk_bf, g_k_nope[j], nh, kvl).reshape(bs, ss * nh, kvl)
            qr_s = qs.reshape(bs, ss, nh, 2 * LANE)[..., LANE:LANE + rope].reshape(bs, ss * nh, rope).astype(BF16)
            pad_keys = lambda t: jnp.pad(t.reshape(bs, ss, -1), ((0, 0), (0, page - ss), (0, 0)))
            o_lat = decode_attn(qlat, qr_s, pad_keys(c[tp:]), pad_keys(krp[tp:, :rope]), pad_keys(ksc[tp:, :nh]),
                                cache_mla_latent, cache_mla_krope, cache_mla_kscale, page_table, j, nh, ss, scale)
            o_s = o_up(o_lat.reshape(ts, nh * kvl), wuv_bf, nh, kvl)
            x = matmul(jnp.concatenate([o_p, o_s], axis=0), w_mla_out[j], res=x, name="mla_out")
            lat_o.append(c)
            kr_o.append(krp[:, :rope])
            ksc_o.append(ksc[:, :nh])
        else:
            proj = matmul(h, w_ssm_in[j], name="ssm_in")
            rep = lambda vv: jnp.repeat(vv, hp).reshape(1, d_inner)
            bias_rep, alog_rep, dskip_rep = rep(dt_bias[j]), rep(a_log[j]), rep(d_skip[j])
            gnorm = g_ssm_norm[j].reshape(1, d_inner)
            dt_raw = proj[:, d_inner + conv_dim:]
            hist0 = jnp.zeros((bp, kc - 1, conv_dim), F32)
            xbc_p = conv_prompt(proj, hist0, conv_w_ssm[j], conv_b_ssm[j], bp, sp, d_inner, conv_dim, False, F32)
            y_p, hl_p = ssd_prompt(xbc_p, proj, jnp.repeat(dt_raw[:tp], hp, axis=1), dt_raw[:tp].T,
                                   bias_rep, alog_rep, dskip_rep, gnorm,
                                   dt_bias[j].reshape(nsh, 1), a_log[j].reshape(nsh, 1), bp, sp, nsh, hp, ns, ng)
            proj_s = proj[tp:].reshape(bs, ss, -1)
            xbc_raw_s = proj_s[..., d_inner:d_inner + conv_dim]
            xbc_s = conv_sample(xbc_raw_s.swapaxes(0, 1), state_ssm_conv[j].swapaxes(0, 1),
                                conv_w_ssm[j], conv_b_ssm[j], conv_dim, False, F32).swapaxes(0, 1)
            padr = lambda t: jnp.pad(t, ((0, 0), (0, lp - ss), (0, 0)))
            y_s, hn_s = ssd_sample(padr(xbc_s[..., :d_inner]), padr(xbc_s[..., d_inner:d_inner + ng * ns]),
                                   padr(xbc_s[..., d_inner + ng * ns:]), padr(proj_s[..., :d_inner]),
                                   padr(jnp.repeat(proj_s[..., d_inner + conv_dim:], hp, axis=2)),
                                   bias_rep, alog_rep, dskip_rep, gnorm,
                                   state_ssm[j].reshape(bs, d_inner, ns), ss, hp, ng)
            y_all = jnp.concatenate([y_p, y_s[:, :ss].reshape(ts, d_inner)], axis=0)
            x = matmul(y_all, w_ssm_out[j], res=x, name="ssm_out")
            ssm_o.append((hl_p.reshape(bp, nsh, hp, ns), hn_s.reshape(bs, nsh, hp, ns)))
            xbc_raw_p = proj[:tp, d_inner:d_inner + conv_dim].reshape(bp, sp, conv_dim)
            sconv_o.append((xbc_raw_p[:, sp - (kc - 1):], xbc_raw_s[:, ss - (kc - 1):]))
        h = rmsnorm(x, norm_ffn[i])
        u = matmul(h, w_ffn_up[i], name="ffn_up")
        g_p = conv_prompt(u, jnp.zeros((bp, kf - 1, 2 * dff), F32), conv_w_ffn[i], conv_b_ffn[i],
                          bp, sp, 0, dff, True, BF16)
        u_s = u[tp:].reshape(bs, ss, 2 * dff)
        g_s = conv_sample(u_s.swapaxes(0, 1), state_ffn_conv[i].swapaxes(0, 1), conv_w_ffn[i], conv_b_ffn[i],
                          dff, True, BF16).swapaxes(0, 1).reshape(ts, dff)
        x = matmul(jnp.concatenate([g_p, g_s], axis=0), w_ffn_down[i], res=x, name="ffn_down")
        fconv_o.append((u[:tp].reshape(bp, sp, 2 * dff)[:, sp - (kf - 1):], u_s[:, ss - (kf - 1):]))

    def split(rows, tail):
        return rows[:tp].reshape((bp, sp) + tail), rows[tp:].reshape((bs, ss) + tail)

    y_p, y_s = split(x, (d,))
    lat = [split(c, (kvl,)) for c in lat_o]
    kr = [split(c, (rope,)) for c in kr_o]
    ksc = [split(c, (nh,)) for c in ksc_o]
    stack = lambda pairs, k: jnp.stack([p[k] for p in pairs])
    return (y_p, y_s,
            stack(lat, 0), stack(kr, 0), stack(ksc, 0), stack(ssm_o, 0), stack(sconv_o, 0), stack(fconv_o, 0),
            stack(lat, 1), stack(kr, 1), stack(ksc, 1), stack(ssm_o, 1), stack(sconv_o, 1), stack(fconv_o, 1))
```

```python
import functools
import math

import jax
import jax.numpy as jnp
from jax import lax
from jax.experimental import pallas as pl
from jax.experimental.pallas import tpu as pltpu

F32, BF16 = jnp.float32, jnp.bfloat16
EPS = 1e-6
ROPE_THETA = 10000.0
SSM_CHUNK = 128
LANE = 128
VMEM_LIMIT = 56 * 1024 * 1024
MM_VMEM_BUDGET = 46 * 1024 * 1024
MM_TK = 1024
MM_MIN_TN = 256
SAMPLE_ROWS_PAD = 16


def _cparams(*sem):
    return pltpu.CompilerParams(dimension_semantics=sem, vmem_limit_bytes=VMEM_LIMIT)


def _pick(n, cands):
    for c in cands:
        if n % c == 0:
            return c
    return n


def _nt(a, b):
    return lax.dot_general(a, b, (((1,), (1,)), ((), ())), preferred_element_type=F32)


def _dot(a, b):
    return jnp.dot(a, b, preferred_element_type=F32)


def _dot_f32(a, b):
    return jnp.dot(a, b, precision=lax.Precision.HIGHEST, preferred_element_type=F32)


def _split3(x):
    hi = x.astype(BF16)
    r1 = x - hi.astype(F32)
    mid = r1.astype(BF16)
    lo = (r1 - mid.astype(F32)).astype(BF16)
    return hi, mid, lo


def _rms(x, g, n):
    ms = jnp.sum(x * x, axis=-1, keepdims=True) * (1.0 / n)
    return (x * lax.rsqrt(ms + EPS)) * g


def _softplus(x):
    return jnp.maximum(x, 0.0) + jnp.log1p(jnp.exp(-jnp.abs(x)))


def _silu(x):
    return x * (1.0 / (1.0 + jnp.exp(-x)))


def _rope128(r, c, s1, s2, half):
    return r * c + pltpu.roll(r, LANE - half, 1) * s1 + pltpu.roll(r, half, 1) * s2


def _rmsnorm_kernel(x_ref, g_ref, o_ref):
    x = x_ref[...]
    o_ref[...] = _rms(x, g_ref[...], x.shape[-1]).astype(o_ref.dtype)


def rmsnorm(x, g, out_dtype=BF16):
    m, d = x.shape
    tm = _pick(m, (256, 128, 64, 32, 16))
    return pl.pallas_call(
        _rmsnorm_kernel,
        grid=(m // tm,),
        in_specs=[pl.BlockSpec((tm, d), lambda i: (i, 0)), pl.BlockSpec((1, d), lambda i: (0, 0))],
        out_specs=pl.BlockSpec((tm, d), lambda i: (i, 0)),
        out_shape=jax.ShapeDtypeStruct((m, d), out_dtype),
        compiler_params=_cparams("parallel"),
        name="rmsnorm",
    )(x, g.reshape(1, d))


def _mm_full_kernel(*refs, has_res):
    if has_res:
        a_ref, w_ref, r_ref, o_ref = refs
    else:
        a_ref, w_ref, o_ref = refs
    r = _dot(a_ref[...].astype(BF16), w_ref[...].astype(BF16))
    if has_res:
        r = r + r_ref[...]
    o_ref[...] = r.astype(o_ref.dtype)


def _mm_ktiled_kernel(*refs, nk, k_rem, has_res):
    if has_res:
        a_ref, w_ref, r_ref, o_ref, acc_ref = refs
    else:
        a_ref, w_ref, o_ref, acc_ref = refs
    k = pl.program_id(2)
    a = a_ref[...]
    w = w_ref[...]
    if k_rem:
        limit = jnp.where(k == nk - 1, k_rem, a.shape[1])
        a = jnp.where(lax.broadcasted_iota(jnp.int32, a.shape, 1) < limit, a, jnp.zeros_like(a))
        w = jnp.where(lax.broadcasted_iota(jnp.int32, w.shape, 0) < limit, w, jnp.zeros_like(w))
    p = _dot(a.astype(BF16), w.astype(BF16))

    @pl.when(k == 0)
    def _():
        acc_ref[...] = p

    @pl.when(k > 0)
    def _():
        acc_ref[...] += p

    @pl.when(k == nk - 1)
    def _():
        r = acc_ref[...]
        if has_res:
            r = r + r_ref[...]
        o_ref[...] = r.astype(o_ref.dtype)


def _mm_plan(m, k, n, a_bytes, has_res):
    tm = _pick(m, (1088, 1024, 512, 256, 128, 64, 32, 16))
    io_tiles = 2 + 2 * has_res
    for tn in (1024, 640, 512, 384, 256):
        full = tm * k * a_bytes + 2 * k * tn * 4 + io_tiles * tm * tn * 4 + k * tn * 2 + tm * tn * 4
        if n % tn == 0 and tn >= MM_MIN_TN and full <= MM_VMEM_BUDGET:
            return tm, tn, k
    return tm, _pick(n, (1024, 512, 256, 128)), min(k, MM_TK)


def matmul(a, w, layer=None, res=None, out_dtype=F32, name="matmul"):
    m, k = a.shape
    n = w.shape[-1]
    assert w.shape[-2] == k
    tm, tn, tk = _mm_plan(m, k, n, a.dtype.itemsize, True)
    if tk != k and k % (2 * LANE) == 0 and _mm_plan(m, k // 2, n, a.dtype.itemsize, True)[2] == k // 2:
        part = _matmul_call(a, w, layer, res, F32, name, k // 2, 0)
        return _matmul_call(a, w, layer, part, out_dtype, name, k // 2, 1)
    return _matmul_call(a, w, layer, res, out_dtype, name, k, 0)


def _matmul_call(a, w, layer, res, out_dtype, name, k, kpart):
    m = a.shape[0]
    n = w.shape[-1]
    tm, tn, tk = _mm_plan(m, k, n, a.dtype.itemsize, res is not None)
    has_res = res is not None
    if tk == k:
        amap, wmap, omap = (lambda i, j: (i, kpart)), (lambda i, j: (kpart, j)), (lambda i, j: (i, j))
        grid, sem, scratch = (m // tm, n // tn), ("parallel", "arbitrary"), []
        body = functools.partial(_mm_full_kernel, has_res=has_res)
        a_spec = pl.BlockSpec((tm, k), amap, pipeline_mode=pl.Buffered(1))
    else:
        assert kpart == 0
        amap, wmap, omap = (lambda i, j, kk: (i, kk)), (lambda i, j, kk: (kk, j)), (lambda i, j, kk: (i, j))
        nk = pl.cdiv(k, tk)
        grid, sem, scratch = (m // tm, n // tn, nk), ("parallel", "parallel", "arbitrary"), [pltpu.VMEM((tm, tn), F32)]
        body = functools.partial(_mm_ktiled_kernel, nk=nk, k_rem=k % tk, has_res=has_res)
        a_spec = pl.BlockSpec((tm, tk), amap)
    if w.ndim == 3:
        w_spec = pl.BlockSpec((None, tk, tn), lambda *g: (layer,) + wmap(*g))
    else:
        w_spec = pl.BlockSpec((tk, tn), wmap)
    in_specs, args = [a_spec, w_spec], [a, w]
    if has_res:
        in_specs.append(pl.BlockSpec((tm, tn), omap))
        args.append(res)
    return pl.pallas_call(
        body,
        grid=grid,
        in_specs=in_specs,
        out_specs=pl.BlockSpec((tm, tn), omap),
        out_shape=jax.ShapeDtypeStruct((m, n), out_dtype),
        scratch_shapes=scratch,
        compiler_params=_cparams(*sem),
        name=name,
    )(*args)


def _mla_post_a_kernel(a_ref, gq_ref, gkv_ref, gkr_ref, c_ref, s1_ref, s2_ref,
                       qa_ref, lat_ref, kr_ref, *, ql, kvl, rope):
    qa_ref[...] = _rms(a_ref[:, :ql], gq_ref[...], ql).astype(qa_ref.dtype)
    lat_ref[...] = _rms(a_ref[:, ql:ql + kvl], gkv_ref[...], kvl)
    r = _rms(a_ref[:, ql + kvl:ql + kvl + LANE], gkr_ref[...], rope)
    kr_ref[...] = _rope128(r, c_ref[...], s1_ref[...], s2_ref[...], rope // 2)


def mla_post_a(a, g_qa, g_kva, g_kr_pad, tabs, ql, kvl, rope):
    t = a.shape[0]
    tm = _pick(t, (256, 128, 64, 32, 16))
    row = lambda w: pl.BlockSpec((tm, w), lambda i: (i, 0))
    vec = lambda w: pl.BlockSpec((1, w), lambda i: (0, 0))
    return pl.pallas_call(
        functools.partial(_mla_post_a_kernel, ql=ql, kvl=kvl, rope=rope),
        grid=(t // tm,),
        in_specs=[row(a.shape[1]), vec(ql), vec(kvl), vec(LANE), row(LANE), row(LANE), row(LANE)],
        out_specs=[row(ql), row(kvl), row(LANE)],
        out_shape=[jax.ShapeDtypeStruct((t, ql), BF16), jax.ShapeDtypeStruct((t, kvl), F32),
                   jax.ShapeDtypeStruct((t, LANE), F32)],
        compiler_params=_cparams("parallel"),
        name="mla_post_a",
    )(a, g_qa.reshape(1, ql), g_kva.reshape(1, kvl), g_kr_pad, *tabs)


def _q_post_kernel(q_ref, gn_ref, gr_ref, c_ref, s1_ref, s2_ref, o_ref, *, hb, nope, rope):
    c, s1, s2 = c_ref[...], s1_ref[...], s2_ref[...]
    for h in range(hb):
        lo = h * 2 * LANE
        qn = _rms(q_ref[:, lo:lo + LANE], gn_ref[...], nope)
        o_ref[:, lo:lo + LANE] = qn.astype(o_ref.dtype)
        r = _rms(q_ref[:, lo + LANE:lo + 2 * LANE], gr_ref[...], rope)
        o_ref[:, lo + LANE:lo + 2 * LANE] = _rope128(r, c, s1, s2, rope // 2).astype(o_ref.dtype)


def q_post(q_raw, g_qn, g_qr_pad, tabs, nope, rope, out_dtype):
    t, w = q_raw.shape
    tm = _pick(t, (256, 128, 64, 32, 16))
    hb = _pick(w // (2 * LANE), (4, 2, 1))
    bw = hb * 2 * LANE
    tab = pl.BlockSpec((tm, LANE), lambda i, j: (i, 0))
    vec = pl.BlockSpec((1, LANE), lambda i, j: (0, 0))
    return pl.pallas_call(
        functools.partial(_q_post_kernel, hb=hb, nope=nope, rope=rope),
        grid=(t // tm, w // bw),
        in_specs=[pl.BlockSpec((tm, bw), lambda i, j: (i, j)), vec, vec, tab, tab, tab],
        out_specs=pl.BlockSpec((tm, bw), lambda i, j: (i, j)),
        out_shape=jax.ShapeDtypeStruct((t, w), out_dtype),
        compiler_params=_cparams("parallel", "parallel"),
        name="q_post",
    )(q_raw, g_qn.reshape(1, LANE), g_qr_pad, *tabs)


def _kv_up_kernel(c_ref, kr_ref, wuk_ref, wuv_ref, gkn_ref, kcat_ref, v_ref, ks_ref, *, nh, nope):
    c = c_ref[...].astype(BF16)
    krb = kr_ref[...].astype(BF16)
    lane = lax.broadcasted_iota(jnp.int32, ks_ref.shape, 1)
    ks_out = jnp.zeros(ks_ref.shape, F32)
    for h in range(nh):
        kn = _dot(c, wuk_ref[:, h * LANE:(h + 1) * LANE])
        ksc = lax.rsqrt(jnp.sum(kn * kn, axis=-1, keepdims=True) * (1.0 / nope) + EPS)
        ks_out = jnp.where(lane == h, ksc, ks_out)
        kcat_ref[:, 2 * h * LANE:(2 * h + 1) * LANE] = ((kn * ksc) * gkn_ref[...]).astype(BF16)
        kcat_ref[:, (2 * h + 1) * LANE:(2 * h + 2) * LANE] = krb
        v_ref[:, h * LANE:(h + 1) * LANE] = _dot(c, wuv_ref[:, h * LANE:(h + 1) * LANE]).astype(BF16)
    ks_ref[...] = ks_out


def kv_up(c, kr_pad, wuk_bf, wuv_bf, g_kn, nh, nope):
    t, kvl = c.shape
    tm = _pick(t, (256, 128, 64, 32, 16))
    row = lambda w: pl.BlockSpec((tm, w), lambda i: (i, 0))
    full = lambda r, w: pl.BlockSpec((r, w), lambda i: (0, 0))
    return pl.pallas_call(
        functools.partial(_kv_up_kernel, nh=nh, nope=nope),
        grid=(t // tm,),
        in_specs=[row(kvl), row(LANE), full(kvl, nh * LANE), full(kvl, nh * LANE), full(1, LANE)],
        out_specs=[row(2 * nh * LANE), row(nh * LANE), row(LANE)],
        out_shape=[jax.ShapeDtypeStruct((t, 2 * nh * LANE), BF16), jax.ShapeDtypeStruct((t, nh * LANE), BF16),
                   jax.ShapeDtypeStruct((t, LANE), F32)],
        compiler_params=_cparams("parallel"),
        name="kv_up",
    )(c, kr_pad, wuk_bf, wuv_bf, g_kn.reshape(1, LANE))


def _flash_kernel(q_ref, k_ref, v_ref, o_ref, *, tq, scale):
    i = pl.program_id(2)
    q = q_ref[...]

    def step(j, carry, diagonal):
        m, l, acc = carry
        off = pl.multiple_of(j * tq, tq)
        s = _nt(q, k_ref[pl.ds(off, tq), :]) * scale
        if diagonal:
            s = jnp.where(lax.broadcasted_iota(jnp.int32, s.shape, 0) >= lax.broadcasted_iota(jnp.int32, s.shape, 1),
                          s, -jnp.inf)
        m_new = jnp.maximum(m, jnp.max(s, axis=-1, keepdims=True))
        alpha = jnp.exp(m - m_new)
        p = jnp.exp(s - m_new)
        l = l * alpha + jnp.sum(p, axis=-1, keepdims=True)
        acc = acc * alpha + _dot(p.astype(BF16), v_ref[pl.ds(off, tq), :])
        return m_new, l, acc

    init = (jnp.full((tq, 1), -jnp.inf, F32), jnp.zeros((tq, 1), F32), jnp.zeros((tq, LANE), F32))
    carry = lax.fori_loop(0, i, lambda j, c: step(j, c, False), init)
    _, l, acc = step(i, carry, True)
    o_ref[...] = (acc / l).astype(o_ref.dtype)


def flash_prompt(qcat, kcat, v, rows_out, bp, sp, nh, scale):
    tq = _pick(sp, (512, 256, 128))
    nq = sp // tq
    return pl.pallas_call(
        functools.partial(_flash_kernel, tq=tq, scale=scale),
        grid=(bp, nh, nq),
        in_specs=[pl.BlockSpec((tq, 2 * LANE), lambda b, h, i: (b * nq + i, h)),
                  pl.BlockSpec((sp, 2 * LANE), lambda b, h, i: (b, h)),
                  pl.BlockSpec((sp, LANE), lambda b, h, i: (b, h))],
        out_specs=pl.BlockSpec((tq, LANE), lambda b, h, i: (b * nq + i, h)),
        out_shape=jax.ShapeDtypeStruct((rows_out, nh * LANE), BF16),
        compiler_params=_cparams("parallel", "parallel", "arbitrary"),
        name="flash_prompt",
    )(qcat, kcat, v)


def _q_absorb_kernel(q_ref, w_ref, g_ref, o_ref):
    qn = (q_ref[:, :LANE] * g_ref[...]).astype(BF16)
    o_ref[...] = _nt(qn, w_ref[...]).astype(o_ref.dtype)


def q_absorb(qs, wuk_bf, g_kn, nh, kvl):
    ts = qs.shape[0]
    return pl.pallas_call(
        _q_absorb_kernel,
        grid=(nh,),
        in_specs=[pl.BlockSpec((ts, 2 * LANE), lambda h: (0, h)),
                  pl.BlockSpec((kvl, LANE), lambda h: (0, h)),
                  pl.BlockSpec((1, LANE), lambda h: (0, 0))],
        out_specs=pl.BlockSpec((ts, kvl), lambda h: (0, h)),
        out_shape=jax.ShapeDtypeStruct((ts, nh * kvl), BF16),
        compiler_params=_cparams("parallel"),
        name="q_absorb",
    )(qs, wuk_bf, g_kn.reshape(1, LANE))


def _o_up_kernel(o_ref, w_ref, out_ref):
    out_ref[...] = _dot(o_ref[...], w_ref[...]).astype(out_ref.dtype)


def o_up(o_lat, wuv_bf, nh, kvl):
    ts = o_lat.shape[0]
    return pl.pallas_call(
        _o_up_kernel,
        grid=(nh,),
        in_specs=[pl.BlockSpec((ts, kvl), lambda h: (0, h)), pl.BlockSpec((kvl, LANE), lambda h: (0, h))],
        out_specs=pl.BlockSpec((ts, LANE), lambda h: (0, h)),
        out_shape=jax.ShapeDtypeStruct((ts, nh * LANE), BF16),
        compiler_params=_cparams("parallel"),
        name="o_up",
    )(o_lat, wuv_bf)


def _decode_kernel(pt_ref, qlat_ref, qr_ref, slat_ref, skr_ref, sks_ref, *rest, pp, nh, tq, scale):
    pages = rest[:3 * pp]
    o_ref = rest[3 * pp]
    m_ref, l_ref, acc_ref = rest[3 * pp + 1:]
    c = pl.program_id(1)
    qlat = qlat_ref[0]
    qr = qr_ref[0]

    def scores(lat_b, kr_t, ks_t):
        ksx = jnp.concatenate([ks_t] * tq, axis=0)
        return (_nt(qlat, lat_b) * ksx + _dot(qr, kr_t.astype(BF16))) * scale

    @pl.when(c == 0)
    def _():
        lat_b = slat_ref[0].astype(BF16)
        s = scores(lat_b, skr_ref[0], sks_ref[0])
        u = lax.broadcasted_iota(jnp.int32, s.shape, 1)
        r = lax.broadcasted_iota(jnp.int32, s.shape, 0)
        s = jnp.where(u * nh <= r, s, -jnp.inf)
        m0 = jnp.max(s, axis=-1, keepdims=True)
        p = jnp.exp(s - m0)
        m_ref[...] = m0
        l_ref[...] = jnp.sum(p, axis=-1, keepdims=True)
        acc_ref[...] = _dot(p.astype(BF16), lat_b)

    lats = [pages[3 * s][...].astype(BF16) for s in range(pp)]
    ss = [scores(lats[s], pages[3 * s + 1][...], pages[3 * s + 2][...]) for s in range(pp)]
    m_prev = m_ref[...]
    m_new = m_prev
    for s in ss:
        m_new = jnp.maximum(m_new, jnp.max(s, axis=-1, keepdims=True))
    alpha = jnp.exp(m_prev - m_new)
    l = l_ref[...] * alpha
    acc = acc_ref[...] * alpha
    for s, lat_b in zip(ss, lats):
        p = jnp.exp(s - m_new)
        l = l + jnp.sum(p, axis=-1, keepdims=True)
        acc = acc + _dot(p.astype(BF16), lat_b)
    m_ref[...] = m_new
    l_ref[...] = l
    acc_ref[...] = acc

    @pl.when(c == pl.num_programs(1) - 1)
    def _():
        o_ref[0] = (acc / l).astype(o_ref.dtype)


def decode_attn(qlat, qr, self_lat, self_kr_t, self_ks_t, cache_lat, cache_kr_t, cache_ks_t, page_table,
                layer, nh, tq, scale):
    bs, rows, kvl = qlat.shape
    n_pages = page_table.shape[1]
    page = cache_lat.shape[2]
    rope = cache_kr_t.shape[2]
    pp = _pick(n_pages, (16, 8, 4, 2, 1))
    nc = n_pages // pp

    def per_b(shape):
        return pl.BlockSpec((1,) + shape, lambda b, c, pt: (b, 0, 0))

    def paged(shape, s):
        return pl.BlockSpec((None, None) + shape, lambda b, c, pt: (layer, pt[b * n_pages + c * pp + s], 0, 0))

    in_specs = [per_b((rows, kvl)), per_b((rows, rope)), per_b((page, kvl)), per_b((rope, page)), per_b((nh, page))]
    args = [qlat, qr, self_lat, self_kr_t, self_ks_t]
    for s in range(pp):
        in_specs += [paged((page, kvl), s), paged((rope, page), s), paged((nh, page), s)]
        args += [cache_lat, cache_kr_t, cache_ks_t]
    return pl.pallas_call(
        functools.partial(_decode_kernel, pp=pp, nh=nh, tq=tq, scale=scale),
        grid_spec=pltpu.PrefetchScalarGridSpec(
            num_scalar_prefetch=1,
            grid=(bs, nc),
            in_specs=in_specs,
            out_specs=pl.BlockSpec((1, rows, kvl), lambda b, c, pt: (b, 0, 0)),
            scratch_shapes=[pltpu.VMEM((rows, 1), F32), pltpu.VMEM((rows, 1), F32), pltpu.VMEM((rows, kvl), F32)],
        ),
        out_shape=jax.ShapeDtypeStruct((bs, rows, kvl), BF16),
        compiler_params=_cparams("parallel", "arbitrary"),
        name="decode_attn",
    )(page_table.reshape(-1), *args)


def _conv_rows(x, hist, w, bias, kf):
    row = lax.broadcasted_iota(jnp.int32, x.shape, 0)
    y = bias + x * w[kf - 1:kf]
    for s in range(1, kf):
        xs = pltpu.roll(x, s, 0)
        for t in range(s):
            xs = jnp.where(row == t, hist[kf - 1 - s + t:kf - s + t], xs)
        y = y + xs * w[kf - 1 - s:kf - s]
    return y


def _conv_prompt_kernel(*refs, kf, gate):
    if gate:
        ug, uv, hg, hv, wg, wv, bg, bv, o_ref = refs
        a = _conv_rows(ug[...], hg[0], wg[...], bg[...], kf)
        b = _conv_rows(uv[...], hv[0], wv[...], bv[...], kf)
        o_ref[...] = (_silu(a) * b).astype(o_ref.dtype)
    else:
        u, h, w, b, o_ref = refs
        o_ref[...] = _silu(_conv_rows(u[...], h[0], w[...], b[...], kf)).astype(o_ref.dtype)


def _conv_col_blocks(col0, width, gate):
    tc = _pick(math.gcd(width, col0) if col0 else width, (256, 128))
    nj = width // tc
    return tc, nj, col0 // tc, ((0, nj) if gate else (0,))


def conv_prompt(u, hist, w, bias, layer, rows_out, nb, sl, col0, width, gate, out_dtype):
    kf = w.shape[1]
    tc, nj, o0, halves = _conv_col_blocks(col0, width, gate)
    u_specs = [pl.BlockSpec((sl, tc), lambda b, j, o=o: (b, o0 + o + j)) for o in halves]
    h_specs = [pl.BlockSpec((1, kf - 1, tc), lambda b, j, o=o: (b, 0, o + j)) for o in halves]
    w_specs = [pl.BlockSpec((None, kf, tc), lambda b, j, o=o: (layer, 0, o + j)) for o in halves]
    b_specs = [pl.BlockSpec((None, 1, tc), lambda b, j, o=o: (layer, 0, o + j)) for o in halves]
    n_in = len(halves)
    return pl.pallas_call(
        functools.partial(_conv_prompt_kernel, kf=kf, gate=gate),
        grid=(nb, nj),
        in_specs=u_specs + h_specs + w_specs + b_specs,
        out_specs=pl.BlockSpec((sl, tc), lambda b, j: (b, j)),
        out_shape=jax.ShapeDtypeStruct((rows_out, width), out_dtype),
        compiler_params=_cparams("parallel", "parallel"),
        name="conv_prompt_gate" if gate else "conv_prompt",
    )(*([u] * n_in + [hist] * n_in + [w] * n_in + [bias[:, None, :]] * n_in))


def _conv_sample_kernel(*refs, kf, ss, bs, gate):
    def conv(u_ref, h_ref, w_ref, b_ref, t):
        y = b_ref[...]
        for j in range(kf):
            src = h_ref[t + j] if t + j < kf - 1 else u_ref[(t + j - (kf - 1)) * bs:(t + j - (kf - 2)) * bs, :]
            y = y + src * w_ref[j:j + 1, :]
        return y

    if gate:
        ug, uv, hg, hv, wg, wv, bg, bv, o_ref = refs
        for t in range(ss):
            o_ref[t * bs:(t + 1) * bs, :] = (_silu(conv(ug, hg, wg, bg, t)) * conv(uv, hv, wv, bv, t)).astype(o_ref.dtype)
    else:
        u, h, w, b, o_ref = refs
        for t in range(ss):
            o_ref[t * bs:(t + 1) * bs, :] = _silu(conv(u, h, w, b, t)).astype(o_ref.dtype)


def conv_sample(u, hist_t, w, bias, layer, row0, ss, bs, col0, width, gate, out_dtype):
    kf = w.shape[1]
    ts = ss * bs
    assert row0 % ts == 0
    rb = row0 // ts
    tc, nj, o0, halves = _conv_col_blocks(col0, width, gate)
    u_specs = [pl.BlockSpec((ts, tc), lambda j, o=o: (rb, o0 + o + j)) for o in halves]
    h_specs = [pl.BlockSpec((kf - 1, bs, tc), lambda j, o=o: (0, 0, o + j)) for o in halves]
    w_specs = [pl.BlockSpec((None, kf, tc), lambda j, o=o: (layer, 0, o + j)) for o in halves]
    b_specs = [pl.BlockSpec((None, 1, tc), lambda j, o=o: (layer, 0, o + j)) for o in halves]
    n_in = len(halves)
    return pl.pallas_call(
        functools.partial(_conv_sample_kernel, kf=kf, ss=ss, bs=bs, gate=gate),
        grid=(nj,),
        in_specs=u_specs + h_specs + w_specs + b_specs,
        out_specs=pl.BlockSpec((ts, tc), lambda j: (0, j)),
        out_shape=jax.ShapeDtypeStruct((ts, width), out_dtype),
        compiler_params=_cparams("parallel"),
        name="conv_sample_gate" if gate else "conv_sample",
    )(*([u] * n_in + [hist_t] * n_in + [w] * n_in + [bias[:, None, :]] * n_in))


def _gated_group_norm(y, z, gn):
    y = y * _silu(z)
    ms = jnp.sum(y * y, axis=-1, keepdims=True) * (1.0 / y.shape[-1])
    return (y * lax.rsqrt(ms + EPS)) * gn


def _ssd_prompt_kernel(xs_ref, bm_ref, cm_ref, z_ref, dt_ref, dtt_ref, bias_ref, alog_ref, dskip_ref, gn_ref,
                       biast_ref, alogt_ref, y_ref, hl_ref, h_ref, *, hp, hg):
    g = pl.program_id(1)
    c = pl.program_id(2)

    @pl.when(c == 0)
    def _():
        h_ref[...] = jnp.zeros_like(h_ref)

    xs = xs_ref[...]
    L, gw = xs.shape
    nh = dt_ref.shape[1]
    hpl = LANE // hp
    bm = bm_ref[...].astype(BF16)
    cm = cm_ref[...].astype(BF16)
    ri = lax.broadcasted_iota(jnp.int32, (L, L), 0)
    ci = lax.broadcasted_iota(jnp.int32, (L, L), 1)
    causal = ri >= ci
    dtc = _softplus(dt_ref[...] + bias_ref[...])
    cumc = _dot_f32(causal.astype(F32), dtc * (-jnp.exp(alog_ref[...])))
    spread = (lax.broadcasted_iota(jnp.int32, (nh, gw), 0)
              == g * hg + lax.broadcasted_iota(jnp.int32, (nh, gw), 1) // hp).astype(BF16)

    def to_lanes(v):
        hi, mid, lo = _split3(v)
        return _dot(hi, spread) + _dot(mid, spread) + _dot(lo, spread)

    dt = to_lanes(dtc)
    cum = to_lanes(cumc)
    dtt = _softplus(dtt_ref[...] + biast_ref[...])
    cumt = _dot_f32(dtt * (-jnp.exp(alogt_ref[...])), (ri <= ci).astype(F32))
    xdt = xs * dt
    cb = _nt(cm, bm)
    cum_last = cum[L - 1:L, :]
    xw = xdt * jnp.exp(cum_last - cum)
    ecum = jnp.exp(cum)
    eye = (ri == ci).astype(BF16)
    lane_head = lax.broadcasted_iota(jnp.int32, (L, LANE), 1) // hp
    row_head = lax.broadcasted_iota(jnp.int32, (LANE, 1), 0) // hp
    ys = []
    for pi in range(gw // LANE):
        sl = slice(pi * LANE, (pi + 1) * LANE)
        xdt_p = xdt[:, sl]
        y_p = jnp.zeros((L, LANE), F32)
        dcol = jnp.zeros((LANE, 1), F32)
        for q in range(hpl):
            k = pi * hpl + q
            colb = jnp.broadcast_to(cum[:, k * hp:k * hp + 1], (L, L))
            rowb = jnp.broadcast_to(cumt[k:k + 1, :], (L, L))
            dec = jnp.where(causal, jnp.exp(colb - rowb), 0.0)
            xk = jnp.where(lane_head == q, xdt_p, 0.0).astype(BF16)
            y_p = y_p + _dot((cb * dec).astype(BF16), xk)
            dcol = jnp.where(row_head == q, jnp.exp(cumt[k:k + 1, L - 1:L]), dcol)
        h_prev = h_ref[sl, :]
        y_p = y_p + _nt(cm, h_prev.astype(BF16)) * ecum[:, sl]
        xwt = _nt(eye, xw[:, sl].astype(BF16))
        h_ref[sl, :] = h_prev * dcol + _dot(xwt.astype(BF16), bm)
        ys.append(y_p)
    y = jnp.concatenate(ys, axis=1) + xs * dskip_ref[...]
    y_ref[...] = _gated_group_norm(y, z_ref[...], gn_ref[...]).astype(y_ref.dtype)

    @pl.when(c == pl.num_programs(2) - 1)
    def _():
        hl_ref[0] = h_ref[...]


def ssd_prompt(xbc, proj, dt_t, dt_bias, a_log, dskip_rep, gnorm, layer, rows_out, bp, sp, nh, hp, ns, ng):
    L = min(SSM_CHUNK, sp)
    nch = sp // L
    d_inner = nh * hp
    gw = d_inner // ng
    hg = nh // ng
    b0 = d_inner // ns
    dtb = (proj.shape[1] - nh) // nh
    assert nh % LANE == 0 and (proj.shape[1] - nh) % nh == 0
    rowb = lambda w, f: pl.BlockSpec((L, w), f)
    lvec = lambda w: pl.BlockSpec((None, 1, w), lambda b, g, c: (layer, 0, 0))
    gvec = pl.BlockSpec((None, 1, gw), lambda b, g, c: (layer, 0, g))
    vect = pl.BlockSpec((None, hg, 1), lambda b, g, c: (layer, g, 0))
    return pl.pallas_call(
        functools.partial(_ssd_prompt_kernel, hp=hp, hg=hg),
        grid=(bp, ng, nch),
        in_specs=[rowb(gw, lambda b, g, c: (b * nch + c, g)),
                  rowb(ns, lambda b, g, c: (b * nch + c, b0 + g)),
                  rowb(ns, lambda b, g, c: (b * nch + c, b0 + ng + g)),
                  rowb(gw, lambda b, g, c: (b * nch + c, g)),
                  rowb(nh, lambda b, g, c: (b * nch + c, dtb)),
                  pl.BlockSpec((hg, L), lambda b, g, c: (g, b * nch + c)),
                  lvec(nh), lvec(nh), gvec, gvec, vect, vect],
        out_specs=[rowb(gw, lambda b, g, c: (b * nch + c, g)),
                   pl.BlockSpec((1, gw, ns), lambda b, g, c: (b, g, 0))],
        out_shape=[jax.ShapeDtypeStruct((rows_out, d_inner), BF16),
                   jax.ShapeDtypeStruct((bp, d_inner, ns), F32)],
        scratch_shapes=[pltpu.VMEM((gw, ns), F32)],
        compiler_params=_cparams("parallel", "parallel", "arbitrary"),
        name="ssd_prompt",
    )(xbc, xbc, xbc, proj, proj, dt_t, dt_bias[:, None, :], a_log[:, None, :], dskip_rep, gnorm,
      dt_bias[:, :, None], a_log[:, :, None])


def _ssd_sample_kernel(*refs, ss, hp, ng, chained):
    if chained:
        refs = refs[1:]
    xs_ref, bm_ref, cm_ref, z_ref, dt_ref, bias_ref, alog_ref, dskip_ref, gn_ref, h0_ref, y_ref, hn_ref = refs
    lp = xs_ref.shape[1]
    gw = xs_ref.shape[2] // ng
    ns = bm_ref.shape[2] // ng
    hpl = LANE // hp
    row = lax.broadcasted_iota(jnp.int32, (lp, gw), 0)
    ri = lax.broadcasted_iota(jnp.int32, (LANE, LANE), 0)
    ci = lax.broadcasted_iota(jnp.int32, (LANE, LANE), 1)
    eye = (ri == ci).astype(BF16)
    row_head = lax.broadcasted_iota(jnp.int32, (LANE, 1), 0) // hp

    def group(g, carry):
        cg = pl.ds(pl.multiple_of(g * gw, gw), gw)
        cn = pl.ds(pl.multiple_of(g * ns, ns), ns)
        xs = xs_ref[0, :, cg]
        bm = bm_ref[0, :, cn]
        cm = cm_ref[0, :, cn]
        dt = jnp.where(row < ss, _softplus(dt_ref[0, :, cg] + bias_ref[:, cg]), 0.0)
        dta = dt * (-jnp.exp(alog_ref[:, cg]))
        cums = [dta[0:1]]
        for l in range(1, ss):
            cums.append(cums[-1] + dta[l:l + 1])
        xdt = xs * dt
        h0 = h0_ref[0, cg, :]
        yoff = _nt(cm.astype(BF16), h0.astype(BF16))
        y = jnp.zeros((lp, gw), F32)
        cum_full = jnp.broadcast_to(cums[ss - 1], (lp, gw))
        for l in range(ss):
            acc = yoff[l:l + 1] * jnp.exp(cums[l]) + xs[l:l + 1] * dskip_ref[:, cg]
            for s in range(l + 1):
                cb = jnp.sum(cm[l:l + 1] * bm[s:s + 1], axis=-1, keepdims=True)
                acc = acc + (cb * jnp.exp(cums[l] - cums[s])) * xdt[s:s + 1]
            y = jnp.where(row == l, acc, y)
            cum_full = jnp.where(row == l, cums[l], cum_full)
        y_ref[0, :, cg] = _gated_group_norm(y, z_ref[0, :, cg], gn_ref[:, cg]).astype(y_ref.dtype)

        xw = xdt * jnp.exp(cums[ss - 1] - cum_full)
        xw_pad = jnp.concatenate([xw, jnp.zeros((LANE - lp, gw), F32)], axis=0).astype(BF16)
        bm_pad = jnp.concatenate([bm, jnp.zeros((LANE - lp, ns), F32)], axis=0).astype(BF16)
        dlast = jnp.exp(cums[ss - 1])
        for pi in range(gw // LANE):
            sl = slice(pi * LANE, (pi + 1) * LANE)
            dcol = jnp.zeros((LANE, 1), F32)
            for q in range(hpl):
                k = pi * hpl + q
                dcol = jnp.where(row_head == q, dlast[0:1, k * hp:k * hp + 1], dcol)
            xwt = _nt(eye, xw_pad[:, sl])
            rows = pl.ds(pl.multiple_of(g * gw + pi * LANE, LANE), LANE)
            hn_ref[0, rows, :] = h0[sl, :] * dcol + _dot(xwt.astype(BF16), bm_pad)
        return carry

    lax.fori_loop(0, ng, group, 0)


def ssd_sample(xs, bm, cm, z, dt_rep, bias_rep, alog_rep, dskip_rep, gnorm, state, state_out, layer, ss, hp, ng):
    bs, lp, d_inner = xs.shape
    ns = state.shape[-1]
    act = lambda w: pl.BlockSpec((1, lp, w), lambda b: (b, 0, 0))
    vec = pl.BlockSpec((None, 1, d_inner), lambda b: (layer, 0, 0))
    st = pl.BlockSpec((None, 1, d_inner, ns), lambda b: (layer, b, 0, 0))
    chained = state_out is not None
    in_specs = [act(d_inner), act(ng * ns), act(ng * ns), act(d_inner), act(d_inner), vec, vec, vec, vec, st]
    args = [xs, bm, cm, z, dt_rep, bias_rep, alog_rep, dskip_rep, gnorm, state]
    if chained:
        in_specs = [pl.BlockSpec(memory_space=pl.ANY)] + in_specs
        args = [state_out] + args
    return pl.pallas_call(
        functools.partial(_ssd_sample_kernel, ss=ss, hp=hp, ng=ng, chained=chained),
        grid=(bs,),
        in_specs=in_specs,
        out_specs=[act(d_inner), st],
        out_shape=[jax.ShapeDtypeStruct((bs, lp, d_inner), BF16), jax.ShapeDtypeStruct(state.shape, F32)],
        input_output_aliases={0: 1} if chained else {},
        compiler_params=_cparams("parallel"),
        name="ssd_sample",
    )(*args)


def kernel(x_prompt, x_sample, cache_mla_latent, cache_mla_krope, cache_mla_kscale, page_table, state_ssm, state_ssm_conv, state_ffn_conv, norm_mix, norm_ffn, w_mla_in, g_q_lora, g_kv_lora, w_uq, w_uk, w_uv, w_mla_out, g_q_nope, g_q_rope, g_k_nope, g_k_rope, w_ssm_in, conv_w_ssm, conv_b_ssm, dt_bias, a_log, d_skip, g_ssm_norm, w_ssm_out, w_ffn_up, conv_w_ffn, conv_b_ffn, w_ffn_down):
    bp, sp, d = x_prompt.shape
    bs, ss, _ = x_sample.shape
    tp, ts = bp * sp, bs * ss
    t_all = tp + ts
    depth = norm_mix.shape[0]
    kvl, nh, nope = w_uk.shape[1:]
    vd = w_uv.shape[3]
    ql = g_q_lora.shape[1]
    rope = g_q_rope.shape[1]
    qk = nope + rope
    assert nope == LANE and vd == LANE and 2 * rope == LANE
    n_pages = page_table.shape[1]
    page = cache_mla_latent.shape[2]
    past_len = n_pages * page
    scale = 1.0 / math.sqrt(qk)
    n_ssm, _, nsh, hp, ns = state_ssm.shape
    d_inner = nsh * hp
    conv_dim = conv_w_ssm.shape[2]
    ng = (conv_dim - d_inner) // (2 * ns)
    kc = conv_w_ssm.shape[1]
    dff = w_ffn_down.shape[1]
    kf = conv_w_ffn.shape[1]
    assert ss >= kc - 1 and ss >= kf - 1 and ss <= SAMPLE_ROWS_PAD and LANE % hp == 0 and ns == LANE
    lp = SAMPLE_ROWS_PAD

    to_bt = lambda r: r.reshape((ss, bs) + r.shape[1:]).swapaxes(0, 1)
    to_tb = lambda r: r.swapaxes(0, 1).reshape((ts,) + r.shape[2:])
    put_sample = lambda buf, rows: lax.dynamic_update_slice(buf, rows, (tp, 0))
    seq_tails = lambda r, n, c0, c1: jnp.stack([r[(b + 1) * sp - n:(b + 1) * sp, c0:c1] for b in range(bp)])

    half = rope // 2
    inv = jnp.exp(-math.log(ROPE_THETA) * jnp.arange(half, dtype=F32) / half)
    pos = jnp.concatenate([jnp.tile(jnp.arange(sp), bp), jnp.repeat(past_len + jnp.arange(ss), bs)])
    ang = pos.astype(F32)[:, None] * inv[None, :]
    cos, sin = jnp.cos(ang), jnp.sin(ang)
    zt = jnp.zeros_like(cos)
    tabs = (jnp.concatenate([cos, cos, zt, zt], 1), jnp.concatenate([-sin, zt, zt, zt], 1),
            jnp.concatenate([zt, sin, zt, zt], 1))
    tabs_s = tuple(t[tp:] for t in tabs)

    def pad_lane(g):
        return jnp.pad(g, (0, LANE - g.shape[0])).reshape(1, LANE)

    cache_kr_t = jnp.swapaxes(cache_mla_krope, 2, 3)
    cache_ks_t = jnp.swapaxes(cache_mla_kscale, 2, 3)
    state4 = state_ssm.reshape(n_ssm, bs, d_inner, ns)
    dskip_rep = jnp.repeat(d_skip, hp, axis=1)[:, None, :]
    bias_rep = jnp.repeat(dt_bias, hp, axis=1)[:, None, :]
    alog_rep = jnp.repeat(a_log, hp, axis=1)[:, None, :]
    gnorm = g_ssm_norm[:, None, :]

    x = jnp.concatenate([x_prompt.reshape(tp, d), to_tb(x_sample)], axis=0)
    lat_o, kr_o, ksc_o, ssm_p_o, sconv_o, fconv_o = [], [], [], [], [], []
    state_new = None

    for i in range(depth):
        j = i // 2
        h = rmsnorm(x, norm_mix[i])
        if i % 2 == 0:
            w_in = jnp.pad(w_mla_in[j], ((0, 0), (0, (-w_mla_in.shape[2]) % 512)))
            a = matmul(h, w_in, name="mla_in")
            qa, c, krp = mla_post_a(a, g_q_lora[j], g_kv_lora[j], pad_lane(g_k_rope[j]), tabs, ql, kvl, rope)
            w_uq_pad = jnp.pad(w_uq[j].reshape(ql, nh, qk), ((0, 0), (0, 0), (0, 2 * LANE - qk))).reshape(ql, nh * 2 * LANE)
            q_raw = matmul(qa, w_uq_pad, name="mla_uq")
            g_qr_pad = pad_lane(g_q_rope[j])
            qcat = q_post(q_raw, g_q_nope[j], g_qr_pad, tabs, nope, rope, BF16)
            wuk_bf = w_uk[j].reshape(kvl, nh * nope).astype(BF16)
            wuv_bf = w_uv[j].reshape(kvl, nh * vd).astype(BF16)
            kcat, v, ksc = kv_up(c, krp, wuk_bf, wuv_bf, g_k_nope[j], nh, nope)
            o = flash_prompt(qcat, kcat, v, t_all, bp, sp, nh, scale)
            qs = q_post(q_raw[tp:], g_q_nope[j], g_qr_pad, tabs_s, nope, rope, F32)
            qlat = to_bt(q_absorb(qs, wuk_bf, g_k_nope[j], nh, kvl)).reshape(bs, ss * nh, kvl)
            qr_s = to_bt(qs).reshape(bs, ss, nh, 2 * LANE)[..., LANE:LANE + rope].reshape(bs, ss * nh, rope).astype(BF16)
            pad_keys = lambda r: jnp.pad(to_bt(r), ((0, 0), (0, page - ss), (0, 0)))
            o_lat = decode_attn(qlat, qr_s, pad_keys(c[tp:]), pad_keys(krp[tp:, :rope]).swapaxes(1, 2),
                                pad_keys(ksc[tp:, :nh]).swapaxes(1, 2), cache_mla_latent, cache_kr_t, cache_ks_t,
                                page_table, j, nh, ss, scale)
            o_s = o_up(to_tb(o_lat.reshape(bs, ss, nh * kvl)), wuv_bf, nh, kvl)
            x = matmul(put_sample(o, o_s), w_mla_out, layer=j, res=x, name="mla_out")
            lat_o.append(c)
            kr_o.append(krp[:, :rope])
            ksc_o.append(ksc[:, :nh])
        else:
            proj = matmul(h, w_ssm_in, layer=j, name="ssm_in")
            hist0 = jnp.zeros((bp, kc - 1, conv_dim), F32)
            xbc_p = conv_prompt(proj, hist0, conv_w_ssm, conv_b_ssm, j, tp, bp, sp, d_inner, conv_dim, False, F32)
            y, hl_p = ssd_prompt(xbc_p, proj, proj[:tp, d_inner + conv_dim:].T, dt_bias, a_log, dskip_rep, gnorm,
                                 j, t_all, bp, sp, nsh, hp, ns, ng)
            xbc_s = conv_sample(proj, state_ssm_conv[j].swapaxes(0, 1), conv_w_ssm, conv_b_ssm, j, tp, ss, bs,
                                d_inner, conv_dim, False, F32)
            proj_s = proj[tp:]
            padr = lambda r: jnp.pad(to_bt(r), ((0, 0), (0, lp - ss), (0, 0)))
            y_s, state_new = ssd_sample(
                padr(xbc_s[:, :d_inner]), padr(xbc_s[:, d_inner:d_inner + ng * ns]), padr(xbc_s[:, d_inner + ng * ns:]),
                padr(proj_s[:, :d_inner]), padr(jnp.repeat(proj_s[:, d_inner + conv_dim:], hp, axis=1)),
                bias_rep, alog_rep, dskip_rep, gnorm, state4, state_new, j, ss, hp, ng)
            x = matmul(put_sample(y, to_tb(y_s[:, :ss])), w_ssm_out, layer=j, res=x, name="ssm_out")
            ssm_p_o.append(hl_p.reshape(bp, nsh, hp, ns))
            sconv_o.append((seq_tails(proj, kc - 1, d_inner, d_inner + conv_dim),
                            to_bt(proj_s[:, d_inner:d_inner + conv_dim])[:, ss - (kc - 1):]))
        h = rmsnorm(x, norm_ffn[i])
        u = matmul(h, w_ffn_up, layer=i, name="ffn_up")
        g = conv_prompt(u, jnp.zeros((bp, kf - 1, 2 * dff), F32), conv_w_ffn, conv_b_ffn, i, t_all,
                        bp, sp, 0, dff, True, BF16)
        g_s = conv_sample(u, state_ffn_conv[i].swapaxes(0, 1), conv_w_ffn, conv_b_ffn, i, tp, ss, bs, 0, dff, True, BF16)
        x = matmul(put_sample(g, g_s), w_ffn_down, layer=i, res=x, name="ffn_down")
        fconv_o.append((seq_tails(u, kf - 1, 0, 2 * dff), to_bt(u[tp:])[:, ss - (kf - 1):]))

    def split(rows, tail):
        return rows[:tp].reshape((bp, sp) + tail), to_bt(rows[tp:])

    y_p, y_s = split(x, (d,))
    lat = [split(c, (kvl,)) for c in lat_o]
    kr = [split(c, (rope,)) for c in kr_o]
    ksc = [split(c, (nh,)) for c in ksc_o]
    stack = lambda pairs, k: jnp.stack([p[k] for p in pairs])
    return (y_p, y_s,
            stack(lat, 0), stack(kr, 0), stack(ksc, 0), jnp.stack(ssm_p_o), stack(sconv_o, 0), stack(fconv_o, 0),
            stack(lat, 1), stack(kr, 1), stack(ksc, 1), state_new.reshape(state_ssm.shape), stack(sconv_o, 1),
            stack(fconv_o, 1))
```

```python
import functools
import math

import jax
import jax.numpy as jnp
from jax import lax
from jax.experimental import pallas as pl
from jax.experimental.pallas import tpu as pltpu

F32, BF16 = jnp.float32, jnp.bfloat16
EPS = 1e-6
ROPE_THETA = 10000.0
SSM_CHUNK = 128
LANE = 128
VMEM_LIMIT = 56 * 1024 * 1024
MM_VMEM_BUDGET = 46 * 1024 * 1024
MM_TK = 1024
MM_MIN_TN = 256
SAMPLE_ROWS_PAD = 16


def _cparams(*sem):
    return pltpu.CompilerParams(dimension_semantics=sem, vmem_limit_bytes=VMEM_LIMIT)


def _pick(n, cands):
    for c in cands:
        if n % c == 0:
            return c
    return n


def _nt(a, b):
    return lax.dot_general(a, b, (((1,), (1,)), ((), ())), preferred_element_type=F32)


def _dot(a, b):
    return jnp.dot(a, b, preferred_element_type=F32)


def _dot_f32(a, b):
    return jnp.dot(a, b, precision=lax.Precision.HIGHEST, preferred_element_type=F32)


def _split3(x):
    hi = x.astype(BF16)
    r1 = x - hi.astype(F32)
    mid = r1.astype(BF16)
    lo = (r1 - mid.astype(F32)).astype(BF16)
    return hi, mid, lo


def _rms(x, g, n):
    ms = jnp.sum(x * x, axis=-1, keepdims=True) * (1.0 / n)
    return (x * lax.rsqrt(ms + EPS)) * g


def _softplus(x):
    return jnp.maximum(x, 0.0) + jnp.log1p(jnp.exp(-jnp.abs(x)))


def _silu(x):
    return x * (1.0 / (1.0 + jnp.exp(-x)))


def _rope128(r, c, s1, s2, half):
    return r * c + pltpu.roll(r, LANE - half, 1) * s1 + pltpu.roll(r, half, 1) * s2


def _rmsnorm_kernel(x_ref, g_ref, o_ref):
    x = x_ref[...]
    o_ref[...] = _rms(x, g_ref[...], x.shape[-1]).astype(o_ref.dtype)


def rmsnorm(x, g, out_dtype=BF16):
    m, d = x.shape
    tm = _pick(m, (256, 128, 64, 32, 16))
    return pl.pallas_call(
        _rmsnorm_kernel,
        grid=(m // tm,),
        in_specs=[pl.BlockSpec((tm, d), lambda i: (i, 0)), pl.BlockSpec((1, d), lambda i: (0, 0))],
        out_specs=pl.BlockSpec((tm, d), lambda i: (i, 0)),
        out_shape=jax.ShapeDtypeStruct((m, d), out_dtype),
        compiler_params=_cparams("parallel"),
        name="rmsnorm",
    )(x, g.reshape(1, d))


def _mm_full_kernel(*refs, has_res):
    if has_res:
        a_ref, w_ref, r_ref, o_ref = refs
    else:
        a_ref, w_ref, o_ref = refs
    r = _dot(a_ref[...].astype(BF16), w_ref[...].astype(BF16))
    if has_res:
        r = r + r_ref[...]
    o_ref[...] = r.astype(o_ref.dtype)


def _mm_ktiled_kernel(*refs, nk, k_rem, has_res):
    if has_res:
        a_ref, w_ref, r_ref, o_ref, acc_ref = refs
    else:
        a_ref, w_ref, o_ref, acc_ref = refs
    k = pl.program_id(2)
    a = a_ref[...]
    w = w_ref[...]
    if k_rem:
        limit = jnp.where(k == nk - 1, k_rem, a.shape[1])
        a = jnp.where(lax.broadcasted_iota(jnp.int32, a.shape, 1) < limit, a, jnp.zeros_like(a))
        w = jnp.where(lax.broadcasted_iota(jnp.int32, w.shape, 0) < limit, w, jnp.zeros_like(w))
    p = _dot(a.astype(BF16), w.astype(BF16))

    @pl.when(k == 0)
    def _():
        acc_ref[...] = p

    @pl.when(k > 0)
    def _():
        acc_ref[...] += p

    @pl.when(k == nk - 1)
    def _():
        r = acc_ref[...]
        if has_res:
            r = r + r_ref[...]
        o_ref[...] = r.astype(o_ref.dtype)


def _mm_plan(m, k, n, a_bytes, has_res):
    tm = _pick(m, (1088, 1024, 512, 256, 128, 64, 32, 16))
    io_tiles = 2 + 2 * has_res
    for tn in (1024, 640, 512, 384, 256):
        full = tm * k * a_bytes + 2 * k * tn * 4 + io_tiles * tm * tn * 4 + k * tn * 2 + tm * tn * 4
        if n % tn == 0 and tn >= MM_MIN_TN and full <= MM_VMEM_BUDGET:
            return tm, tn, k
    return tm, _pick(n, (1024, 512, 256, 128)), min(k, MM_TK)


def matmul(a, w, layer=None, res=None, out_dtype=F32, name="matmul"):
    m, k = a.shape
    n = w.shape[-1]
    assert w.shape[-2] == k
    tm, tn, tk = _mm_plan(m, k, n, a.dtype.itemsize, res is not None)
    if tk != k and k % (2 * LANE) == 0 and _mm_plan(m, k // 2, n, a.dtype.itemsize, True)[2] == k // 2:
        part = _matmul_call(a, w, layer, res, F32, name, k // 2, 0)
        return _matmul_call(a, w, layer, part, out_dtype, name, k // 2, 1)
    return _matmul_call(a, w, layer, res, out_dtype, name, k, 0)


def _matmul_call(a, w, layer, res, out_dtype, name, k, kpart):
    m = a.shape[0]
    n = w.shape[-1]
    tm, tn, tk = _mm_plan(m, k, n, a.dtype.itemsize, res is not None)
    has_res = res is not None
    if tk == k:
        amap, wmap, omap = (lambda i, j: (i, kpart)), (lambda i, j: (kpart, j)), (lambda i, j: (i, j))
        grid, sem, scratch = (m // tm, n // tn), ("parallel", "arbitrary"), []
        body = functools.partial(_mm_full_kernel, has_res=has_res)
        a_spec = pl.BlockSpec((tm, k), amap, pipeline_mode=pl.Buffered(1))
    else:
        assert kpart == 0
        amap, wmap, omap = (lambda i, j, kk: (i, kk)), (lambda i, j, kk: (kk, j)), (lambda i, j, kk: (i, j))
        nk = pl.cdiv(k, tk)
        grid, sem, scratch = (m // tm, n // tn, nk), ("parallel", "parallel", "arbitrary"), [pltpu.VMEM((tm, tn), F32)]
        body = functools.partial(_mm_ktiled_kernel, nk=nk, k_rem=k % tk, has_res=has_res)
        a_spec = pl.BlockSpec((tm, tk), amap)
    if w.ndim == 3:
        w_spec = pl.BlockSpec((None, tk, tn), lambda *g: (layer,) + wmap(*g))
    else:
        w_spec = pl.BlockSpec((tk, tn), wmap)
    in_specs, args = [a_spec, w_spec], [a, w]
    if has_res:
        in_specs.append(pl.BlockSpec((tm, tn), omap))
        args.append(res)
    return pl.pallas_call(
        body,
        grid=grid,
        in_specs=in_specs,
        out_specs=pl.BlockSpec((tm, tn), omap),
        out_shape=jax.ShapeDtypeStruct((m, n), out_dtype),
        scratch_shapes=scratch,
        compiler_params=_cparams(*sem),
        name=name,
    )(*args)


def _mla_post_a_kernel(a_ref, gq_ref, gkv_ref, gkr_ref, c_ref, s1_ref, s2_ref,
                       qa_ref, lat_ref, kr_ref, *, ql, kvl, rope):
    qa_ref[...] = _rms(a_ref[:, :ql], gq_ref[...], ql).astype(qa_ref.dtype)
    lat_ref[...] = _rms(a_ref[:, ql:ql + kvl], gkv_ref[...], kvl)
    r = _rms(a_ref[:, ql + kvl:ql + kvl + LANE], gkr_ref[...], rope)
    kr_ref[...] = _rope128(r, c_ref[...], s1_ref[...], s2_ref[...], rope // 2)


def mla_post_a(a, g_qa, g_kva, g_kr_pad, tabs, ql, kvl, rope):
    t = a.shape[0]
    tm = _pick(t, (256, 128, 64, 32, 16))
    row = lambda w: pl.BlockSpec((tm, w), lambda i: (i, 0))
    vec = lambda w: pl.BlockSpec((1, w), lambda i: (0, 0))
    return pl.pallas_call(
        functools.partial(_mla_post_a_kernel, ql=ql, kvl=kvl, rope=rope),
        grid=(t // tm,),
        in_specs=[row(a.shape[1]), vec(ql), vec(kvl), vec(LANE), row(LANE), row(LANE), row(LANE)],
        out_specs=[row(ql), row(kvl), row(LANE)],
        out_shape=[jax.ShapeDtypeStruct((t, ql), BF16), jax.ShapeDtypeStruct((t, kvl), F32),
                   jax.ShapeDtypeStruct((t, LANE), F32)],
        compiler_params=_cparams("parallel"),
        name="mla_post_a",
    )(a, g_qa.reshape(1, ql), g_kva.reshape(1, kvl), g_kr_pad, *tabs)


def _q_post_kernel(q_ref, gn_ref, gr_ref, c_ref, s1_ref, s2_ref, o_ref, *, hb, nope, rope):
    c, s1, s2 = c_ref[...], s1_ref[...], s2_ref[...]
    for h in range(hb):
        lo = h * 2 * LANE
        qn = _rms(q_ref[:, lo:lo + LANE], gn_ref[...], nope)
        o_ref[:, lo:lo + LANE] = qn.astype(o_ref.dtype)
        r = _rms(q_ref[:, lo + LANE:lo + 2 * LANE], gr_ref[...], rope)
        o_ref[:, lo + LANE:lo + 2 * LANE] = _rope128(r, c, s1, s2, rope // 2).astype(o_ref.dtype)


def q_post(q_raw, g_qn, g_qr_pad, tabs, nope, rope, out_dtype):
    t, w = q_raw.shape
    tm = _pick(t, (256, 128, 64, 32, 16))
    hb = _pick(w // (2 * LANE), (4, 2, 1))
    bw = hb * 2 * LANE
    tab = pl.BlockSpec((tm, LANE), lambda i, j: (i, 0))
    vec = pl.BlockSpec((1, LANE), lambda i, j: (0, 0))
    return pl.pallas_call(
        functools.partial(_q_post_kernel, hb=hb, nope=nope, rope=rope),
        grid=(t // tm, w // bw),
        in_specs=[pl.BlockSpec((tm, bw), lambda i, j: (i, j)), vec, vec, tab, tab, tab],
        out_specs=pl.BlockSpec((tm, bw), lambda i, j: (i, j)),
        out_shape=jax.ShapeDtypeStruct((t, w), out_dtype),
        compiler_params=_cparams("parallel", "parallel"),
        name="q_post",
    )(q_raw, g_qn.reshape(1, LANE), g_qr_pad, *tabs)


def _kv_up_kernel(c_ref, kr_ref, wuk_ref, wuv_ref, gkn_ref, kcat_ref, v_ref, ks_ref, *, nh, nope):
    c = c_ref[...].astype(BF16)
    krb = kr_ref[...].astype(BF16)
    lane = lax.broadcasted_iota(jnp.int32, ks_ref.shape, 1)
    ks_out = jnp.zeros(ks_ref.shape, F32)
    for h in range(nh):
        kn = _dot(c, wuk_ref[:, h * LANE:(h + 1) * LANE])
        ksc = lax.rsqrt(jnp.sum(kn * kn, axis=-1, keepdims=True) * (1.0 / nope) + EPS)
        ks_out = jnp.where(lane == h, ksc, ks_out)
        kcat_ref[:, 2 * h * LANE:(2 * h + 1) * LANE] = ((kn * ksc) * gkn_ref[...]).astype(BF16)
        kcat_ref[:, (2 * h + 1) * LANE:(2 * h + 2) * LANE] = krb
        v_ref[:, h * LANE:(h + 1) * LANE] = _dot(c, wuv_ref[:, h * LANE:(h + 1) * LANE]).astype(BF16)
    ks_ref[...] = ks_out


def kv_up(c, kr_pad, wuk_bf, wuv_bf, g_kn, nh, nope):
    t, kvl = c.shape
    tm = _pick(t, (256, 128, 64, 32, 16))
    row = lambda w: pl.BlockSpec((tm, w), lambda i: (i, 0))
    full = lambda r, w: pl.BlockSpec((r, w), lambda i: (0, 0))
    return pl.pallas_call(
        functools.partial(_kv_up_kernel, nh=nh, nope=nope),
        grid=(t // tm,),
        in_specs=[row(kvl), row(LANE), full(kvl, nh * LANE), full(kvl, nh * LANE), full(1, LANE)],
        out_specs=[row(2 * nh * LANE), row(nh * LANE), row(LANE)],
        out_shape=[jax.ShapeDtypeStruct((t, 2 * nh * LANE), BF16), jax.ShapeDtypeStruct((t, nh * LANE), BF16),
                   jax.ShapeDtypeStruct((t, LANE), F32)],
        compiler_params=_cparams("parallel"),
        name="kv_up",
    )(c, kr_pad, wuk_bf, wuv_bf, g_kn.reshape(1, LANE))


def _flash_kernel(q_ref, k_ref, v_ref, o_ref, *, tq, scale):
    i = pl.program_id(2)
    q = q_ref[...]

    def step(j, carry, diagonal):
        m, l, acc = carry
        off = pl.multiple_of(j * tq, tq)
        s = _nt(q, k_ref[pl.ds(off, tq), :]) * scale
        if diagonal:
            s = jnp.where(lax.broadcasted_iota(jnp.int32, s.shape, 0) >= lax.broadcasted_iota(jnp.int32, s.shape, 1),
                          s, -jnp.inf)
        m_new = jnp.maximum(m, jnp.max(s, axis=-1, keepdims=True))
        alpha = jnp.exp(m - m_new)
        p = jnp.exp(s - m_new)
        l = l * alpha + jnp.sum(p, axis=-1, keepdims=True)
        acc = acc * alpha + _dot(p.astype(BF16), v_ref[pl.ds(off, tq), :])
        return m_new, l, acc

    init = (jnp.full((tq, 1), -jnp.inf, F32), jnp.zeros((tq, 1), F32), jnp.zeros((tq, LANE), F32))
    carry = lax.fori_loop(0, i, lambda j, c: step(j, c, False), init)
    _, l, acc = step(i, carry, True)
    o_ref[...] = (acc / l).astype(o_ref.dtype)


def flash_prompt(qcat, kcat, v, rows_out, bp, sp, nh, scale):
    tq = _pick(sp, (512, 256, 128))
    nq = sp // tq
    return pl.pallas_call(
        functools.partial(_flash_kernel, tq=tq, scale=scale),
        grid=(bp, nh, nq),
        in_specs=[pl.BlockSpec((tq, 2 * LANE), lambda b, h, i: (b * nq + i, h)),
                  pl.BlockSpec((sp, 2 * LANE), lambda b, h, i: (b, h)),
                  pl.BlockSpec((sp, LANE), lambda b, h, i: (b, h))],
        out_specs=pl.BlockSpec((tq, LANE), lambda b, h, i: (b * nq + i, h)),
        out_shape=jax.ShapeDtypeStruct((rows_out, nh * LANE), BF16),
        compiler_params=_cparams("parallel", "parallel", "arbitrary"),
        name="flash_prompt",
    )(qcat, kcat, v)


def _q_absorb_kernel(q_ref, w_ref, g_ref, o_ref):
    qn = (q_ref[:, :LANE] * g_ref[...]).astype(BF16)
    o_ref[...] = _nt(qn, w_ref[...]).astype(o_ref.dtype)


def q_absorb(qs, wuk_bf, g_kn, nh, kvl):
    ts = qs.shape[0]
    return pl.pallas_call(
        _q_absorb_kernel,
        grid=(nh,),
        in_specs=[pl.BlockSpec((ts, 2 * LANE), lambda h: (0, h)),
                  pl.BlockSpec((kvl, LANE), lambda h: (0, h)),
                  pl.BlockSpec((1, LANE), lambda h: (0, 0))],
        out_specs=pl.BlockSpec((ts, kvl), lambda h: (0, h)),
        out_shape=jax.ShapeDtypeStruct((ts, nh * kvl), BF16),
        compiler_params=_cparams("parallel"),
        name="q_absorb",
    )(qs, wuk_bf, g_kn.reshape(1, LANE))


def _o_up_kernel(o_ref, w_ref, out_ref):
    out_ref[...] = _dot(o_ref[...], w_ref[...]).astype(out_ref.dtype)


def o_up(o_lat, wuv_bf, nh, kvl):
    ts = o_lat.shape[0]
    return pl.pallas_call(
        _o_up_kernel,
        grid=(nh,),
        in_specs=[pl.BlockSpec((ts, kvl), lambda h: (0, h)), pl.BlockSpec((kvl, LANE), lambda h: (0, h))],
        out_specs=pl.BlockSpec((ts, LANE), lambda h: (0, h)),
        out_shape=jax.ShapeDtypeStruct((ts, nh * LANE), BF16),
        compiler_params=_cparams("parallel"),
        name="o_up",
    )(o_lat, wuv_bf)


def _decode_kernel(pt_ref, qlat_ref, qr_ref, slat_ref, skr_ref, sks_ref, *rest, pp, nh, tq, scale):
    pages = rest[:3 * pp]
    o_ref = rest[3 * pp]
    m_ref, l_ref, acc_ref = rest[3 * pp + 1:]
    c = pl.program_id(1)
    qlat = qlat_ref[0]
    qr = qr_ref[0]

    def scores(lat_b, kr_t, ks_t):
        ksx = jnp.concatenate([ks_t] * tq, axis=0)
        return (_nt(qlat, lat_b) * ksx + _dot(qr, kr_t.astype(BF16))) * scale

    @pl.when(c == 0)
    def _():
        lat_b = slat_ref[0].astype(BF16)
        s = scores(lat_b, skr_ref[0], sks_ref[0])
        u = lax.broadcasted_iota(jnp.int32, s.shape, 1)
        r = lax.broadcasted_iota(jnp.int32, s.shape, 0)
        s = jnp.where(u * nh <= r, s, -jnp.inf)
        m0 = jnp.max(s, axis=-1, keepdims=True)
        p = jnp.exp(s - m0)
        m_ref[...] = m0
        l_ref[...] = jnp.sum(p, axis=-1, keepdims=True)
        acc_ref[...] = _dot(p.astype(BF16), lat_b)

    grp = 2 if pp % 2 == 0 else 1
    cat = lambda kind, s0, axis: jnp.concatenate([pages[3 * s + kind][...] for s in range(s0, s0 + grp)], axis=axis)
    lats = [cat(0, s0, 0).astype(BF16) for s0 in range(0, pp, grp)]
    ss = [scores(lats[s0 // grp], cat(1, s0, 1), cat(2, s0, 1)) for s0 in range(0, pp, grp)]
    m_prev = m_ref[...]
    m_new = m_prev
    for s in ss:
        m_new = jnp.maximum(m_new, jnp.max(s, axis=-1, keepdims=True))
    alpha = jnp.exp(m_prev - m_new)
    l = l_ref[...] * alpha
    acc = acc_ref[...] * alpha
    for s, lat_b in zip(ss, lats):
        p = jnp.exp(s - m_new)
        l = l + jnp.sum(p, axis=-1, keepdims=True)
        acc = acc + _dot(p.astype(BF16), lat_b)
    m_ref[...] = m_new
    l_ref[...] = l
    acc_ref[...] = acc

    @pl.when(c == pl.num_programs(1) - 1)
    def _():
        o_ref[0] = (acc / l).astype(o_ref.dtype)


def decode_attn(qlat, qr, self_lat, self_kr_t, self_ks_t, cache_lat, cache_kr_t, cache_ks_t, page_table,
                layer, nh, tq, scale):
    bs, rows, kvl = qlat.shape
    n_pages = page_table.shape[1]
    page = cache_lat.shape[2]
    rope = cache_kr_t.shape[2]
    pp = _pick(n_pages, (16, 8, 4, 2, 1))
    nc = n_pages // pp

    def per_b(shape):
        return pl.BlockSpec((1,) + shape, lambda b, c, pt: (b, 0, 0))

    def paged(shape, s):
        return pl.BlockSpec((None, None) + shape, lambda b, c, pt: (layer, pt[b * n_pages + c * pp + s], 0, 0))

    in_specs = [per_b((rows, kvl)), per_b((rows, rope)), per_b((page, kvl)), per_b((rope, page)), per_b((nh, page))]
    args = [qlat, qr, self_lat, self_kr_t, self_ks_t]
    for s in range(pp):
        in_specs += [paged((page, kvl), s), paged((rope, page), s), paged((nh, page), s)]
        args += [cache_lat, cache_kr_t, cache_ks_t]
    return pl.pallas_call(
        functools.partial(_decode_kernel, pp=pp, nh=nh, tq=tq, scale=scale),
        grid_spec=pltpu.PrefetchScalarGridSpec(
            num_scalar_prefetch=1,
            grid=(bs, nc),
            in_specs=in_specs,
            out_specs=pl.BlockSpec((1, rows, kvl), lambda b, c, pt: (b, 0, 0)),
            scratch_shapes=[pltpu.VMEM((rows, 1), F32), pltpu.VMEM((rows, 1), F32), pltpu.VMEM((rows, kvl), F32)],
        ),
        out_shape=jax.ShapeDtypeStruct((bs, rows, kvl), BF16),
        compiler_params=_cparams("parallel", "arbitrary"),
        name="decode_attn",
    )(page_table.reshape(-1), *args)


def _conv_rows(x, hist, w, bias, kf):
    row = lax.broadcasted_iota(jnp.int32, x.shape, 0)
    y = bias + x * w[kf - 1:kf]
    for s in range(1, kf):
        xs = pltpu.roll(x, s, 0)
        for t in range(s):
            xs = jnp.where(row == t, hist[kf - 1 - s + t:kf - s + t], xs)
        y = y + xs * w[kf - 1 - s:kf - s]
    return y


def _conv_prompt_kernel(*refs, kf, gate):
    if gate:
        ug, uv, hg, hv, wg, wv, bg, bv, o_ref = refs
        a = _conv_rows(ug[...], hg[0], wg[...], bg[...], kf)
        b = _conv_rows(uv[...], hv[0], wv[...], bv[...], kf)
        o_ref[...] = (_silu(a) * b).astype(o_ref.dtype)
    else:
        u, h, w, b, o_ref = refs
        o_ref[...] = _silu(_conv_rows(u[...], h[0], w[...], b[...], kf)).astype(o_ref.dtype)


def _conv_col_blocks(col0, width, gate):
    tc = _pick(math.gcd(width, col0) if col0 else width, (256, 128))
    nj = width // tc
    return tc, nj, col0 // tc, ((0, nj) if gate else (0,))


def conv_prompt(u, hist, w, bias, layer, rows_out, nb, sl, col0, width, gate, out_dtype):
    kf = w.shape[1]
    tc, nj, o0, halves = _conv_col_blocks(col0, width, gate)
    u_specs = [pl.BlockSpec((sl, tc), lambda b, j, o=o: (b, o0 + o + j)) for o in halves]
    h_specs = [pl.BlockSpec((1, kf - 1, tc), lambda b, j, o=o: (b, 0, o + j)) for o in halves]
    w_specs = [pl.BlockSpec((None, kf, tc), lambda b, j, o=o: (layer, 0, o + j)) for o in halves]
    b_specs = [pl.BlockSpec((None, 1, tc), lambda b, j, o=o: (layer, 0, o + j)) for o in halves]
    n_in = len(halves)
    return pl.pallas_call(
        functools.partial(_conv_prompt_kernel, kf=kf, gate=gate),
        grid=(nb, nj),
        in_specs=u_specs + h_specs + w_specs + b_specs,
        out_specs=pl.BlockSpec((sl, tc), lambda b, j: (b, j)),
        out_shape=jax.ShapeDtypeStruct((rows_out, width), out_dtype),
        compiler_params=_cparams("parallel", "parallel"),
        name="conv_prompt_gate" if gate else "conv_prompt",
    )(*([u] * n_in + [hist] * n_in + [w] * n_in + [bias[:, None, :]] * n_in))


def _conv_sample_kernel(*refs, kf, ss, bs, gate):
    def conv(u_ref, h_ref, w_ref, b_ref, t):
        y = b_ref[...]
        for j in range(kf):
            src = h_ref[t + j] if t + j < kf - 1 else u_ref[(t + j - (kf - 1)) * bs:(t + j - (kf - 2)) * bs, :]
            y = y + src * w_ref[j:j + 1, :]
        return y

    if gate:
        ug, uv, hg, hv, wg, wv, bg, bv, o_ref = refs
        for t in range(ss):
            o_ref[t * bs:(t + 1) * bs, :] = (_silu(conv(ug, hg, wg, bg, t)) * conv(uv, hv, wv, bv, t)).astype(o_ref.dtype)
    else:
        u, h, w, b, o_ref = refs
        for t in range(ss):
            o_ref[t * bs:(t + 1) * bs, :] = _silu(conv(u, h, w, b, t)).astype(o_ref.dtype)


def conv_sample(u, hist_t, w, bias, layer, row0, ss, bs, col0, width, gate, out_dtype):
    kf = w.shape[1]
    ts = ss * bs
    assert row0 % ts == 0
    rb = row0 // ts
    tc, nj, o0, halves = _conv_col_blocks(col0, width, gate)
    u_specs = [pl.BlockSpec((ts, tc), lambda j, o=o: (rb, o0 + o + j)) for o in halves]
    h_specs = [pl.BlockSpec((kf - 1, bs, tc), lambda j, o=o: (0, 0, o + j)) for o in halves]
    w_specs = [pl.BlockSpec((None, kf, tc), lambda j, o=o: (layer, 0, o + j)) for o in halves]
    b_specs = [pl.BlockSpec((None, 1, tc), lambda j, o=o: (layer, 0, o + j)) for o in halves]
    n_in = len(halves)
    return pl.pallas_call(
        functools.partial(_conv_sample_kernel, kf=kf, ss=ss, bs=bs, gate=gate),
        grid=(nj,),
        in_specs=u_specs + h_specs + w_specs + b_specs,
        out_specs=pl.BlockSpec((ts, tc), lambda j: (0, j)),
        out_shape=jax.ShapeDtypeStruct((ts, width), out_dtype),
        compiler_params=_cparams("parallel"),
        name="conv_sample_gate" if gate else "conv_sample",
    )(*([u] * n_in + [hist_t] * n_in + [w] * n_in + [bias[:, None, :]] * n_in))


HALO_ROWS = 16
TAIL_ROWS = 8


def _ffn_up_prompt_kernel(a_ref, halo_ref, wg_ref, wv_ref, cwg_ref, cwv_ref, bg_ref, bv_ref,
                          g_ref, tg_ref, tv_ref, *, kf, tiles_per_seq):
    i = pl.program_id(0)
    a = a_ref[...]
    halo = halo_ref[...]
    keep = jnp.where(i % tiles_per_seq == 0, 0.0, 1.0)
    tm = a.shape[0]

    def half(w_ref, cw_ref, b_ref, t_ref):
        w = w_ref[...].astype(BF16)
        u = _dot(a, w)
        hist = _dot(halo, w)[HALO_ROWS - (kf - 1):] * keep
        t_ref[...] = u[tm - TAIL_ROWS:]
        return _conv_rows(u, hist, cw_ref[...], b_ref[...], kf)

    g_ref[...] = (_silu(half(wg_ref, cwg_ref, bg_ref, tg_ref)) * half(wv_ref, cwv_ref, bv_ref, tv_ref)).astype(g_ref.dtype)


def _ffn_up_sample_kernel(a_ref, hg_ref, hv_ref, wg_ref, wv_ref, cwg_ref, cwv_ref, bg_ref, bv_ref,
                          g_ref, tg_ref, tv_ref, *, kf, ss, bs):
    a = a_ref[...]

    def half(w_ref, h_ref, cw_ref, b_ref, t_ref):
        u = _dot(a, w_ref[...].astype(BF16))
        t_ref[...] = u[(ss - (kf - 1)) * bs:]
        ys = []
        for t in range(ss):
            y = b_ref[...]
            for j in range(kf):
                src = h_ref[t + j] if t + j < kf - 1 else u[(t + j - (kf - 1)) * bs:(t + j - (kf - 2)) * bs]
                y = y + src * cw_ref[j:j + 1, :]
            ys.append(y)
        return ys

    yg = half(wg_ref, hg_ref, cwg_ref, bg_ref, tg_ref)
    yv = half(wv_ref, hv_ref, cwv_ref, bv_ref, tv_ref)
    for t in range(ss):
        g_ref[t * bs:(t + 1) * bs, :] = (_silu(yg[t]) * yv[t]).astype(g_ref.dtype)


def ffn_up_gate(h, hist_t, w_up, conv_w, conv_b, layer, bp, sp, ss, bs):
    t_all, d = h.shape
    tp, ts = bp * sp, ss * bs
    dff = w_up.shape[2] // 2
    kf = conv_w.shape[1]
    tc = _pick(dff, (256, 128))
    nj = dff // tc
    tm = _pick(sp, (1024, 512, 256, 128, 64, 32, 16))
    nt = tp // tm
    assert tm % HALO_ROWS == 0 and tp % ts == 0 and kf - 1 <= TAIL_ROWS

    def cols(shape, first_dims, o):
        return pl.BlockSpec(shape, lambda *g, o=o: first_dims + (o + g[-1],))

    def wspecs(o):
        return [cols((None, d, tc), (layer, 0), o)]

    def cspecs(o):
        return [cols((None, kf, tc), (layer, 0), o), cols((None, 1, tc), (layer, 0), o)]

    cb = conv_b[:, None, :]
    g, tg, tv = pl.pallas_call(
        functools.partial(_ffn_up_prompt_kernel, kf=kf, tiles_per_seq=sp // tm),
        grid=(nt, nj),
        in_specs=[pl.BlockSpec((tm, d), lambda i, j: (i, 0), pipeline_mode=pl.Buffered(1)),
                  pl.BlockSpec((HALO_ROWS, d), lambda i, j: (jnp.maximum(i * (tm // HALO_ROWS) - 1, 0), 0))]
                 + wspecs(0) + wspecs(nj) + [cspecs(0)[0], cspecs(nj)[0], cspecs(0)[1], cspecs(nj)[1]],
        out_specs=[pl.BlockSpec((tm, tc), lambda i, j: (i, j)),
                   pl.BlockSpec((TAIL_ROWS, tc), lambda i, j: (i, j)),
                   pl.BlockSpec((TAIL_ROWS, tc), lambda i, j: (i, j))],
        out_shape=[jax.ShapeDtypeStruct((t_all, dff), BF16),
                   jax.ShapeDtypeStruct((nt * TAIL_ROWS, dff), F32),
                   jax.ShapeDtypeStruct((nt * TAIL_ROWS, dff), F32)],
        compiler_params=_cparams("parallel", "arbitrary"),
        name="ffn_up_prompt",
    )(h, h, w_up, w_up, conv_w, conv_w, cb, cb)
    per_seq = lambda r: r.reshape(nt, TAIL_ROWS, dff)[sp // tm - 1::sp // tm, TAIL_ROWS - (kf - 1):]
    tails_p = jnp.concatenate([per_seq(tg), per_seq(tv)], axis=-1)

    hspec = lambda o: [cols((kf - 1, bs, tc), (0, 0), o)]
    g_s, sg, sv = pl.pallas_call(
        functools.partial(_ffn_up_sample_kernel, kf=kf, ss=ss, bs=bs),
        grid=(nj,),
        in_specs=[pl.BlockSpec((ts, d), lambda j: (tp // ts, 0))] + hspec(0) + hspec(nj)
                 + wspecs(0) + wspecs(nj) + [cspecs(0)[0], cspecs(nj)[0], cspecs(0)[1], cspecs(nj)[1]],
        out_specs=[pl.BlockSpec((ts, tc), lambda j: (0, j)),
                   pl.BlockSpec(((kf - 1) * bs, tc), lambda j: (0, j)),
                   pl.BlockSpec(((kf - 1) * bs, tc), lambda j: (0, j))],
        out_shape=[jax.ShapeDtypeStruct((ts, dff), BF16),
                   jax.ShapeDtypeStruct(((kf - 1) * bs, dff), F32),
                   jax.ShapeDtypeStruct(((kf - 1) * bs, dff), F32)],
        compiler_params=_cparams("arbitrary"),
        name="ffn_up_sample",
    )(h, hist_t, hist_t, w_up, w_up, conv_w, conv_w, cb, cb)
    tails_s = jnp.concatenate([sg, sv], axis=-1).reshape(kf - 1, bs, 2 * dff).swapaxes(0, 1)
    return g, g_s, tails_p, tails_s


def _gated_group_norm(y, z, gn):
    y = y * _silu(z)
    ms = jnp.sum(y * y, axis=-1, keepdims=True) * (1.0 / y.shape[-1])
    return (y * lax.rsqrt(ms + EPS)) * gn


def _ssd_prompt_kernel(xs_ref, bm_ref, cm_ref, z_ref, dt_ref, dtt_ref, bias_ref, alog_ref, dskip_ref, gn_ref,
                       biast_ref, alogt_ref, y_ref, hl_ref, h_ref, *, hp, hg):
    g = pl.program_id(1)
    c = pl.program_id(2)

    @pl.when(c == 0)
    def _():
        h_ref[...] = jnp.zeros_like(h_ref)

    xs = xs_ref[...]
    L, gw = xs.shape
    nh = dt_ref.shape[1]
    hpl = LANE // hp
    bm = bm_ref[...].astype(BF16)
    cm = cm_ref[...].astype(BF16)
    ri = lax.broadcasted_iota(jnp.int32, (L, L), 0)
    ci = lax.broadcasted_iota(jnp.int32, (L, L), 1)
    causal = ri >= ci
    dtc = _softplus(dt_ref[...] + bias_ref[...])
    cumc = _dot_f32(causal.astype(F32), dtc * (-jnp.exp(alog_ref[...])))
    spread = (lax.broadcasted_iota(jnp.int32, (nh, gw), 0)
              == g * hg + lax.broadcasted_iota(jnp.int32, (nh, gw), 1) // hp).astype(BF16)

    def to_lanes(v):
        hi, mid, lo = _split3(v)
        return _dot(hi, spread) + _dot(mid, spread) + _dot(lo, spread)

    dt = to_lanes(dtc)
    cum = to_lanes(cumc)
    dtt = _softplus(dtt_ref[...] + biast_ref[...])
    cumt = _dot_f32(dtt * (-jnp.exp(alogt_ref[...])), (ri <= ci).astype(F32))
    xdt = xs * dt
    cb = _nt(cm, bm)
    cum_last = cum[L - 1:L, :]
    xw = xdt * jnp.exp(cum_last - cum)
    ecum = jnp.exp(cum)
    eye = (ri == ci).astype(BF16)
    lane_head = lax.broadcasted_iota(jnp.int32, (L, LANE), 1) // hp
    row_head = lax.broadcasted_iota(jnp.int32, (LANE, 1), 0) // hp
    ys = []
    for pi in range(gw // LANE):
        sl = slice(pi * LANE, (pi + 1) * LANE)
        xdt_p = xdt[:, sl]
        y_p = jnp.zeros((L, LANE), F32)
        dcol = jnp.zeros((LANE, 1), F32)
        for q in range(hpl):
            k = pi * hpl + q
            colb = jnp.broadcast_to(cum[:, k * hp:k * hp + 1], (L, L))
            rowb = jnp.broadcast_to(cumt[k:k + 1, :], (L, L))
            dec = jnp.where(causal, jnp.exp(colb - rowb), 0.0)
            xk = jnp.where(lane_head == q, xdt_p, 0.0).astype(BF16)
            y_p = y_p + _dot((cb * dec).astype(BF16), xk)
            dcol = jnp.where(row_head == q, jnp.exp(cumt[k:k + 1, L - 1:L]), dcol)
        h_prev = h_ref[sl, :]
        y_p = y_p + _nt(cm, h_prev.astype(BF16)) * ecum[:, sl]
        xwt = _nt(eye, xw[:, sl].astype(BF16))
        h_ref[sl, :] = h_prev * dcol + _dot(xwt.astype(BF16), bm)
        ys.append(y_p)
    y = jnp.concatenate(ys, axis=1) + xs * dskip_ref[...]
    y_ref[...] = _gated_group_norm(y, z_ref[...], gn_ref[...]).astype(y_ref.dtype)

    @pl.when(c == pl.num_programs(2) - 1)
    def _():
        hl_ref[0] = h_ref[...]


def ssd_prompt(xbc, proj, dt_t, dt_bias, a_log, dskip_rep, gnorm, layer, rows_out, bp, sp, nh, hp, ns, ng):
    L = min(SSM_CHUNK, sp)
    nch = sp // L
    d_inner = nh * hp
    gw = d_inner // ng
    hg = nh // ng
    b0 = d_inner // ns
    dtb = (proj.shape[1] - nh) // nh
    assert nh % LANE == 0 and (proj.shape[1] - nh) % nh == 0
    rowb = lambda w, f: pl.BlockSpec((L, w), f)
    lvec = lambda w: pl.BlockSpec((None, 1, w), lambda b, g, c: (layer, 0, 0))
    gvec = pl.BlockSpec((None, 1, gw), lambda b, g, c: (layer, 0, g))
    vect = pl.BlockSpec((None, hg, 1), lambda b, g, c: (layer, g, 0))
    return pl.pallas_call(
        functools.partial(_ssd_prompt_kernel, hp=hp, hg=hg),
        grid=(bp, ng, nch),
        in_specs=[rowb(gw, lambda b, g, c: (b * nch + c, g)),
                  rowb(ns, lambda b, g, c: (b * nch + c, b0 + g)),
                  rowb(ns, lambda b, g, c: (b * nch + c, b0 + ng + g)),
                  rowb(gw, lambda b, g, c: (b * nch + c, g)),
                  rowb(nh, lambda b, g, c: (b * nch + c, dtb)),
                  pl.BlockSpec((hg, L), lambda b, g, c: (g, b * nch + c)),
                  lvec(nh), lvec(nh), gvec, gvec, vect, vect],
        out_specs=[rowb(gw, lambda b, g, c: (b * nch + c, g)),
                   pl.BlockSpec((1, gw, ns), lambda b, g, c: (b, g, 0))],
        out_shape=[jax.ShapeDtypeStruct((rows_out, d_inner), BF16),
                   jax.ShapeDtypeStruct((bp, d_inner, ns), F32)],
        scratch_shapes=[pltpu.VMEM((gw, ns), F32)],
        compiler_params=_cparams("parallel", "parallel", "arbitrary"),
        name="ssd_prompt",
    )(xbc, xbc, xbc, proj, proj, dt_t, dt_bias[:, None, :], a_log[:, None, :], dskip_rep, gnorm,
      dt_bias[:, :, None], a_log[:, :, None])


def _ssd_sample_kernel(*refs, ss, hp, ng, chained):
    if chained:
        refs = refs[1:]
    xs_ref, bm_ref, cm_ref, z_ref, dt_ref, bias_ref, alog_ref, dskip_ref, gn_ref, h0_ref, y_ref, hn_ref = refs
    lp = xs_ref.shape[1]
    gw = xs_ref.shape[2] // ng
    ns = bm_ref.shape[2] // ng
    hpl = LANE // hp
    row = lax.broadcasted_iota(jnp.int32, (lp, gw), 0)
    ri = lax.broadcasted_iota(jnp.int32, (LANE, LANE), 0)
    ci = lax.broadcasted_iota(jnp.int32, (LANE, LANE), 1)
    eye = (ri == ci).astype(BF16)
    row_head = lax.broadcasted_iota(jnp.int32, (LANE, 1), 0) // hp

    def group(g, carry):
        cg = pl.ds(pl.multiple_of(g * gw, gw), gw)
        cn = pl.ds(pl.multiple_of(g * ns, ns), ns)
        xs = xs_ref[0, :, cg]
        bm = bm_ref[0, :, cn]
        cm = cm_ref[0, :, cn]
        dt = jnp.where(row < ss, _softplus(dt_ref[0, :, cg] + bias_ref[:, cg]), 0.0)
        dta = dt * (-jnp.exp(alog_ref[:, cg]))
        cums = [dta[0:1]]
        for l in range(1, ss):
            cums.append(cums[-1] + dta[l:l + 1])
        xdt = xs * dt
        h0 = h0_ref[0, cg, :]
        yoff = _nt(cm.astype(BF16), h0.astype(BF16))
        y = jnp.zeros((lp, gw), F32)
        cum_full = jnp.broadcast_to(cums[ss - 1], (lp, gw))
        for l in range(ss):
            acc = yoff[l:l + 1] * jnp.exp(cums[l]) + xs[l:l + 1] * dskip_ref[:, cg]
            for s in range(l + 1):
                cb = jnp.sum(cm[l:l + 1] * bm[s:s + 1], axis=-1, keepdims=True)
                acc = acc + (cb * jnp.exp(cums[l] - cums[s])) * xdt[s:s + 1]
            y = jnp.where(row == l, acc, y)
            cum_full = jnp.where(row == l, cums[l], cum_full)
        y_ref[0, :, cg] = _gated_group_norm(y, z_ref[0, :, cg], gn_ref[:, cg]).astype(y_ref.dtype)

        xw = xdt * jnp.exp(cums[ss - 1] - cum_full)
        xw_pad = jnp.concatenate([xw, jnp.zeros((LANE - lp, gw), F32)], axis=0).astype(BF16)
        bm_pad = jnp.concatenate([bm, jnp.zeros((LANE - lp, ns), F32)], axis=0).astype(BF16)
        dlast = jnp.exp(cums[ss - 1])
        for pi in range(gw // LANE):
            sl = slice(pi * LANE, (pi + 1) * LANE)
            dcol = jnp.zeros((LANE, 1), F32)
            for q in range(hpl):
                k = pi * hpl + q
                dcol = jnp.where(row_head == q, dlast[0:1, k * hp:k * hp + 1], dcol)
            xwt = _nt(eye, xw_pad[:, sl])
            rows = pl.ds(pl.multiple_of(g * gw + pi * LANE, LANE), LANE)
            hn_ref[0, rows, :] = h0[sl, :] * dcol + _dot(xwt.astype(BF16), bm_pad)
        return carry

    lax.fori_loop(0, ng, group, 0)


def ssd_sample(xs, bm, cm, z, dt_rep, bias_rep, alog_rep, dskip_rep, gnorm, state, state_out, layer, ss, hp, ng):
    bs, lp, d_inner = xs.shape
    ns = state.shape[-1]
    act = lambda w: pl.BlockSpec((1, lp, w), lambda b: (b, 0, 0))
    vec = pl.BlockSpec((None, 1, d_inner), lambda b: (layer, 0, 0))
    st = pl.BlockSpec((None, 1, d_inner, ns), lambda b: (layer, b, 0, 0))
    chained = state_out is not None
    in_specs = [act(d_inner), act(ng * ns), act(ng * ns), act(d_inner), act(d_inner), vec, vec, vec, vec, st]
    args = [xs, bm, cm, z, dt_rep, bias_rep, alog_rep, dskip_rep, gnorm, state]
    if chained:
        in_specs = [pl.BlockSpec(memory_space=pl.ANY)] + in_specs
        args = [state_out] + args
    return pl.pallas_call(
        functools.partial(_ssd_sample_kernel, ss=ss, hp=hp, ng=ng, chained=chained),
        grid=(bs,),
        in_specs=in_specs,
        out_specs=[act(d_inner), st],
        out_shape=[jax.ShapeDtypeStruct((bs, lp, d_inner), BF16), jax.ShapeDtypeStruct(state.shape, F32)],
        input_output_aliases={0: 1} if chained else {},
        compiler_params=_cparams("parallel"),
        name="ssd_sample",
    )(*args)


def kernel(x_prompt, x_sample, cache_mla_latent, cache_mla_krope, cache_mla_kscale, page_table, state_ssm, state_ssm_conv, state_ffn_conv, norm_mix, norm_ffn, w_mla_in, g_q_lora, g_kv_lora, w_uq, w_uk, w_uv, w_mla_out, g_q_nope, g_q_rope, g_k_nope, g_k_rope, w_ssm_in, conv_w_ssm, conv_b_ssm, dt_bias, a_log, d_skip, g_ssm_norm, w_ssm_out, w_ffn_up, conv_w_ffn, conv_b_ffn, w_ffn_down):
    bp, sp, d = x_prompt.shape
    bs, ss, _ = x_sample.shape
    tp, ts = bp * sp, bs * ss
    t_all = tp + ts
    depth = norm_mix.shape[0]
    kvl, nh, nope = w_uk.shape[1:]
    vd = w_uv.shape[3]
    ql = g_q_lora.shape[1]
    rope = g_q_rope.shape[1]
    qk = nope + rope
    assert nope == LANE and vd == LANE and 2 * rope == LANE
    n_pages = page_table.shape[1]
    page = cache_mla_latent.shape[2]
    past_len = n_pages * page
    scale = 1.0 / math.sqrt(qk)
    n_ssm, _, nsh, hp, ns = state_ssm.shape
    d_inner = nsh * hp
    conv_dim = conv_w_ssm.shape[2]
    ng = (conv_dim - d_inner) // (2 * ns)
    kc = conv_w_ssm.shape[1]
    dff = w_ffn_down.shape[1]
    kf = conv_w_ffn.shape[1]
    assert ss >= kc - 1 and ss >= kf - 1 and ss <= SAMPLE_ROWS_PAD and LANE % hp == 0 and ns == LANE
    lp = SAMPLE_ROWS_PAD

    to_bt = lambda r: r.reshape((ss, bs) + r.shape[1:]).swapaxes(0, 1)
    to_tb = lambda r: r.swapaxes(0, 1).reshape((ts,) + r.shape[2:])
    put_sample = lambda buf, rows: lax.dynamic_update_slice(buf, rows, (tp, 0))
    seq_tails = lambda r, n, c0, c1: jnp.stack([r[(b + 1) * sp - n:(b + 1) * sp, c0:c1] for b in range(bp)])

    half = rope // 2
    inv = jnp.exp(-math.log(ROPE_THETA) * jnp.arange(half, dtype=F32) / half)
    pos = jnp.concatenate([jnp.tile(jnp.arange(sp), bp), jnp.repeat(past_len + jnp.arange(ss), bs)])
    ang = pos.astype(F32)[:, None] * inv[None, :]
    cos, sin = jnp.cos(ang), jnp.sin(ang)
    zt = jnp.zeros_like(cos)
    tabs = (jnp.concatenate([cos, cos, zt, zt], 1), jnp.concatenate([-sin, zt, zt, zt], 1),
            jnp.concatenate([zt, sin, zt, zt], 1))
    tabs_s = tuple(t[tp:] for t in tabs)

    def pad_lane(g):
        return jnp.pad(g, (0, LANE - g.shape[0])).reshape(1, LANE)

    cache_kr_t = jnp.swapaxes(cache_mla_krope, 2, 3)
    cache_ks_t = jnp.swapaxes(cache_mla_kscale, 2, 3)
    state4 = state_ssm.reshape(n_ssm, bs, d_inner, ns)
    dskip_rep = jnp.repeat(d_skip, hp, axis=1)[:, None, :]
    bias_rep = jnp.repeat(dt_bias, hp, axis=1)[:, None, :]
    alog_rep = jnp.repeat(a_log, hp, axis=1)[:, None, :]
    gnorm = g_ssm_norm[:, None, :]

    x = jnp.concatenate([x_prompt.reshape(tp, d), to_tb(x_sample)], axis=0)
    lat_o, kr_o, ksc_o, ssm_p_o, sconv_o, fconv_o = [], [], [], [], [], []
    state_new = None

    for i in range(depth):
        j = i // 2
        h = rmsnorm(x, norm_mix[i])
        if i % 2 == 0:
            w_in = jnp.pad(w_mla_in[j], ((0, 0), (0, (-w_mla_in.shape[2]) % 512)))
            a = matmul(h, w_in, name="mla_in")
            qa, c, krp = mla_post_a(a, g_q_lora[j], g_kv_lora[j], pad_lane(g_k_rope[j]), tabs, ql, kvl, rope)
            w_uq_pad = jnp.pad(w_uq[j].reshape(ql, nh, qk), ((0, 0), (0, 0), (0, 2 * LANE - qk))).reshape(ql, nh * 2 * LANE)
            q_raw = matmul(qa, w_uq_pad, name="mla_uq")
            g_qr_pad = pad_lane(g_q_rope[j])
            qcat = q_post(q_raw, g_q_nope[j], g_qr_pad, tabs, nope, rope, BF16)
            wuk_bf = w_uk[j].reshape(kvl, nh * nope).astype(BF16)
            wuv_bf = w_uv[j].reshape(kvl, nh * vd).astype(BF16)
            kcat, v, ksc = kv_up(c, krp, wuk_bf, wuv_bf, g_k_nope[j], nh, nope)
            o = flash_prompt(qcat, kcat, v, t_all, bp, sp, nh, scale)
            qs = q_post(q_raw[tp:], g_q_nope[j], g_qr_pad, tabs_s, nope, rope, F32)
            qlat = to_bt(q_absorb(qs, wuk_bf, g_k_nope[j], nh, kvl)).reshape(bs, ss * nh, kvl)
            qr_s = to_bt(qs).reshape(bs, ss, nh, 2 * LANE)[..., LANE:LANE + rope].reshape(bs, ss * nh, rope).astype(BF16)
            pad_keys = lambda r: jnp.pad(to_bt(r), ((0, 0), (0, page - ss), (0, 0)))
            o_lat = decode_attn(qlat, qr_s, pad_keys(c[tp:]), pad_keys(krp[tp:, :rope]).swapaxes(1, 2),
                                pad_keys(ksc[tp:, :nh]).swapaxes(1, 2), cache_mla_latent, cache_kr_t, cache_ks_t,
                                page_table, j, nh, ss, scale)
            o_s = o_up(to_tb(o_lat.reshape(bs, ss, nh * kvl)), wuv_bf, nh, kvl)
            x = matmul(put_sample(o, o_s), w_mla_out, layer=j, res=x, name="mla_out")
            lat_o.append(c)
            kr_o.append(krp[:, :rope])
            ksc_o.append(ksc[:, :nh])
        else:
            proj = matmul(h, w_ssm_in, layer=j, name="ssm_in")
            hist0 = jnp.zeros((bp, kc - 1, conv_dim), F32)
            xbc_p = conv_prompt(proj, hist0, conv_w_ssm, conv_b_ssm, j, tp, bp, sp, d_inner, conv_dim, False, F32)
            y, hl_p = ssd_prompt(xbc_p, proj, proj[:tp, d_inner + conv_dim:].T, dt_bias, a_log, dskip_rep, gnorm,
                                 j, t_all, bp, sp, nsh, hp, ns, ng)
            xbc_s = conv_sample(proj, state_ssm_conv[j].swapaxes(0, 1), conv_w_ssm, conv_b_ssm, j, tp, ss, bs,
                                d_inner, conv_dim, False, F32)
            proj_s = proj[tp:]
            padr = lambda r: jnp.pad(to_bt(r), ((0, 0), (0, lp - ss), (0, 0)))
            y_s, state_new = ssd_sample(
                padr(xbc_s[:, :d_inner]), padr(xbc_s[:, d_inner:d_inner + ng * ns]), padr(xbc_s[:, d_inner + ng * ns:]),
                padr(proj_s[:, :d_inner]), padr(jnp.repeat(proj_s[:, d_inner + conv_dim:], hp, axis=1)),
                bias_rep, alog_rep, dskip_rep, gnorm, state4, state_new, j, ss, hp, ng)
            x = matmul(put_sample(y, to_tb(y_s[:, :ss])), w_ssm_out, layer=j, res=x, name="ssm_out")
            ssm_p_o.append(hl_p.reshape(bp, nsh, hp, ns))
            sconv_o.append((seq_tails(proj, kc - 1, d_inner, d_inner + conv_dim),
                            to_bt(proj_s[:, d_inner:d_inner + conv_dim])[:, ss - (kc - 1):]))
        h = rmsnorm(x, norm_ffn[i])
        g, g_s, tails_p, tails_s = ffn_up_gate(h, state_ffn_conv[i].swapaxes(0, 1), w_ffn_up, conv_w_ffn, conv_b_ffn,
                                               i, bp, sp, ss, bs)
        x = matmul(put_sample(g, g_s), w_ffn_down, layer=i, res=x, name="ffn_down")
        fconv_o.append((tails_p, tails_s))

    def split(rows, tail):
        return rows[:tp].reshape((bp, sp) + tail), to_bt(rows[tp:])

    y_p, y_s = split(x, (d,))
    lat = [split(c, (kvl,)) for c in lat_o]
    kr = [split(c, (rope,)) for c in kr_o]
    ksc = [split(c, (nh,)) for c in ksc_o]
    stack = lambda pairs, k: jnp.stack([p[k] for p in pairs])
    return (y_p, y_s,
            stack(lat, 0), stack(kr, 0), stack(ksc, 0), jnp.stack(ssm_p_o), stack(sconv_o, 0), stack(fconv_o, 0),
            stack(lat, 1), stack(kr, 1), stack(ksc, 1), state_new.reshape(state_ssm.shape), stack(sconv_o, 1),
            stack(fconv_o, 1))
```

```python
import functools
import math

import jax
import jax.numpy as jnp
from jax import lax
from jax.experimental import pallas as pl
from jax.experimental.pallas import tpu as pltpu

F32, BF16 = jnp.float32, jnp.bfloat16
EPS = 1e-6
ROPE_THETA = 10000.0
SSM_CHUNK = 128
LANE = 128
VMEM_LIMIT = 56 * 1024 * 1024
MM_VMEM_BUDGET = 46 * 1024 * 1024
MM_TK = 1024
MM_MIN_TN = 256
SAMPLE_ROWS_PAD = 16
DECODE_PAGE_GROUP = 2


def _cparams(*sem):
    return pltpu.CompilerParams(dimension_semantics=sem, vmem_limit_bytes=VMEM_LIMIT)


def _pick(n, cands):
    for c in cands:
        if n % c == 0:
            return c
    return n


def _nt(a, b):
    return lax.dot_general(a, b, (((1,), (1,)), ((), ())), preferred_element_type=F32)


def _dot(a, b):
    return jnp.dot(a, b, preferred_element_type=F32)


def _dot_f32(a, b):
    return jnp.dot(a, b, precision=lax.Precision.HIGHEST, preferred_element_type=F32)


def _split3(x):
    hi = x.astype(BF16)
    r1 = x - hi.astype(F32)
    mid = r1.astype(BF16)
    lo = (r1 - mid.astype(F32)).astype(BF16)
    return hi, mid, lo


def _rms(x, g, n):
    ms = jnp.sum(x * x, axis=-1, keepdims=True) * (1.0 / n)
    return (x * lax.rsqrt(ms + EPS)) * g


def _softplus(x):
    return jnp.maximum(x, 0.0) + jnp.log1p(jnp.exp(-jnp.abs(x)))


def _silu(x):
    return x * (1.0 / (1.0 + jnp.exp(-x)))


def _rope128(r, c, s1, s2, half):
    return r * c + pltpu.roll(r, LANE - half, 1) * s1 + pltpu.roll(r, half, 1) * s2


def _rmsnorm_kernel(x_ref, g_ref, o_ref):
    x = x_ref[...]
    o_ref[...] = _rms(x, g_ref[...], x.shape[-1]).astype(o_ref.dtype)


def rmsnorm(x, g, out_dtype=BF16):
    m, d = x.shape
    tm = _pick(m, (256, 128, 64, 32, 16))
    return pl.pallas_call(
        _rmsnorm_kernel,
        grid=(m // tm,),
        in_specs=[pl.BlockSpec((tm, d), lambda i: (i, 0)), pl.BlockSpec((1, d), lambda i: (0, 0))],
        out_specs=pl.BlockSpec((tm, d), lambda i: (i, 0)),
        out_shape=jax.ShapeDtypeStruct((m, d), out_dtype),
        compiler_params=_cparams("parallel"),
        name="rmsnorm",
    )(x, g.reshape(1, d))


def _mm_full_kernel(*refs, has_res):
    if has_res:
        a_ref, w_ref, r_ref, o_ref = refs
    else:
        a_ref, w_ref, o_ref = refs
    r = _dot(a_ref[...].astype(BF16), w_ref[...].astype(BF16))
    if has_res:
        r = r + r_ref[...]
    o_ref[...] = r.astype(o_ref.dtype)


def _mm_ktiled_kernel(*refs, nk, k_rem, has_res):
    if has_res:
        a_ref, w_ref, r_ref, o_ref, acc_ref = refs
    else:
        a_ref, w_ref, o_ref, acc_ref = refs
    k = pl.program_id(2)
    a = a_ref[...]
    w = w_ref[...]
    if k_rem:
        limit = jnp.where(k == nk - 1, k_rem, a.shape[1])
        a = jnp.where(lax.broadcasted_iota(jnp.int32, a.shape, 1) < limit, a, jnp.zeros_like(a))
        w = jnp.where(lax.broadcasted_iota(jnp.int32, w.shape, 0) < limit, w, jnp.zeros_like(w))
    p = _dot(a.astype(BF16), w.astype(BF16))

    @pl.when(k == 0)
    def _():
        acc_ref[...] = p

    @pl.when(k > 0)
    def _():
        acc_ref[...] += p

    @pl.when(k == nk - 1)
    def _():
        r = acc_ref[...]
        if has_res:
            r = r + r_ref[...]
        o_ref[...] = r.astype(o_ref.dtype)


def _mm_plan(m, k, n, a_bytes, has_res):
    tm = _pick(m, (1088, 1024, 512, 256, 128, 64, 32, 16))
    io_tiles = 2 + 2 * has_res
    for tn in (1024, 640, 512, 384, 256):
        full = tm * k * a_bytes + 2 * k * tn * 4 + io_tiles * tm * tn * 4 + k * tn * 2 + tm * tn * 4
        if n % tn == 0 and tn >= MM_MIN_TN and full <= MM_VMEM_BUDGET:
            return tm, tn, k
    return tm, _pick(n, (1024, 512, 256, 128)), min(k, MM_TK)


def matmul(a, w, layer=None, res=None, out_dtype=F32, name="matmul"):
    m, k = a.shape
    n = w.shape[-1]
    assert w.shape[-2] == k
    tm, tn, tk = _mm_plan(m, k, n, a.dtype.itemsize, res is not None)
    if tk != k and k % (2 * LANE) == 0 and _mm_plan(m, k // 2, n, a.dtype.itemsize, True)[2] == k // 2:
        part = _matmul_call(a, w, layer, res, F32, name, k // 2, 0)
        return _matmul_call(a, w, layer, part, out_dtype, name, k // 2, 1)
    return _matmul_call(a, w, layer, res, out_dtype, name, k, 0)


def _matmul_call(a, w, layer, res, out_dtype, name, k, kpart):
    m = a.shape[0]
    n = w.shape[-1]
    tm, tn, tk = _mm_plan(m, k, n, a.dtype.itemsize, res is not None)
    has_res = res is not None
    if tk == k:
        amap, wmap, omap = (lambda i, j: (i, kpart)), (lambda i, j: (kpart, j)), (lambda i, j: (i, j))
        grid, sem, scratch = (m // tm, n // tn), ("parallel", "arbitrary"), []
        body = functools.partial(_mm_full_kernel, has_res=has_res)
        a_spec = pl.BlockSpec((tm, k), amap, pipeline_mode=pl.Buffered(1))
    else:
        assert kpart == 0
        amap, wmap, omap = (lambda i, j, kk: (i, kk)), (lambda i, j, kk: (kk, j)), (lambda i, j, kk: (i, j))
        nk = pl.cdiv(k, tk)
        grid, sem, scratch = (m // tm, n // tn, nk), ("parallel", "parallel", "arbitrary"), [pltpu.VMEM((tm, tn), F32)]
        body = functools.partial(_mm_ktiled_kernel, nk=nk, k_rem=k % tk, has_res=has_res)
        a_spec = pl.BlockSpec((tm, tk), amap)
    if w.ndim == 3:
        w_spec = pl.BlockSpec((None, tk, tn), lambda *g: (layer,) + wmap(*g))
    else:
        w_spec = pl.BlockSpec((tk, tn), wmap)
    in_specs, args = [a_spec, w_spec], [a, w]
    if has_res:
        in_specs.append(pl.BlockSpec((tm, tn), omap))
        args.append(res)
    return pl.pallas_call(
        body,
        grid=grid,
        in_specs=in_specs,
        out_specs=pl.BlockSpec((tm, tn), omap),
        out_shape=jax.ShapeDtypeStruct((m, n), out_dtype),
        scratch_shapes=scratch,
        compiler_params=_cparams(*sem),
        name=name,
    )(*args)


def _mla_post_a_kernel(a_ref, gq_ref, gkv_ref, gkr_ref, c_ref, s1_ref, s2_ref,
                       qa_ref, lat_ref, kr_ref, *, ql, kvl, rope):
    qa_ref[...] = _rms(a_ref[:, :ql], gq_ref[...], ql).astype(qa_ref.dtype)
    lat_ref[...] = _rms(a_ref[:, ql:ql + kvl], gkv_ref[...], kvl)
    r = _rms(a_ref[:, ql + kvl:ql + kvl + LANE], gkr_ref[...], rope)
    kr_ref[...] = _rope128(r, c_ref[...], s1_ref[...], s2_ref[...], rope // 2)


def mla_post_a(a, g_qa, g_kva, g_kr_pad, tabs, ql, kvl, rope):
    t = a.shape[0]
    tm = _pick(t, (256, 128, 64, 32, 16))
    row = lambda w: pl.BlockSpec((tm, w), lambda i: (i, 0))
    vec = lambda w: pl.BlockSpec((1, w), lambda i: (0, 0))
    return pl.pallas_call(
        functools.partial(_mla_post_a_kernel, ql=ql, kvl=kvl, rope=rope),
        grid=(t // tm,),
        in_specs=[row(a.shape[1]), vec(ql), vec(kvl), vec(LANE), row(LANE), row(LANE), row(LANE)],
        out_specs=[row(ql), row(kvl), row(LANE)],
        out_shape=[jax.ShapeDtypeStruct((t, ql), BF16), jax.ShapeDtypeStruct((t, kvl), F32),
                   jax.ShapeDtypeStruct((t, LANE), F32)],
        compiler_params=_cparams("parallel"),
        name="mla_post_a",
    )(a, g_qa.reshape(1, ql), g_kva.reshape(1, kvl), g_kr_pad, *tabs)


def _q_post_kernel(q_ref, gn_ref, gr_ref, c_ref, s1_ref, s2_ref, o_ref, *, hb, nope, rope):
    c, s1, s2 = c_ref[...], s1_ref[...], s2_ref[...]
    parts = []
    for h in range(hb):
        lo = h * 2 * LANE
        parts.append(_rms(q_ref[:, lo:lo + LANE], gn_ref[...], nope).astype(o_ref.dtype))
        r = _rms(q_ref[:, lo + LANE:lo + 2 * LANE], gr_ref[...], rope)
        parts.append(_rope128(r, c, s1, s2, rope // 2).astype(o_ref.dtype))
    o_ref[...] = jnp.concatenate(parts, axis=1)


def q_post(q_raw, g_qn, g_qr_pad, tabs, nope, rope, out_dtype):
    t, w = q_raw.shape
    tm = _pick(t, (256, 128, 64, 32, 16))
    hb = _pick(w // (2 * LANE), (4, 2, 1))
    bw = hb * 2 * LANE
    tab = pl.BlockSpec((tm, LANE), lambda i, j: (i, 0))
    vec = pl.BlockSpec((1, LANE), lambda i, j: (0, 0))
    return pl.pallas_call(
        functools.partial(_q_post_kernel, hb=hb, nope=nope, rope=rope),
        grid=(t // tm, w // bw),
        in_specs=[pl.BlockSpec((tm, bw), lambda i, j: (i, j)), vec, vec, tab, tab, tab],
        out_specs=pl.BlockSpec((tm, bw), lambda i, j: (i, j)),
        out_shape=jax.ShapeDtypeStruct((t, w), out_dtype),
        compiler_params=_cparams("parallel", "parallel"),
        name="q_post",
    )(q_raw, g_qn.reshape(1, LANE), g_qr_pad, *tabs)


def _kv_up_kernel(c_ref, kr_ref, wuk_ref, wuv_ref, gkn_ref, kcat_ref, v_ref, ks_ref, *, nh, nope):
    c = c_ref[...].astype(BF16)
    krb = kr_ref[...].astype(BF16)
    lane = lax.broadcasted_iota(jnp.int32, ks_ref.shape, 1)
    ks_out = jnp.zeros(ks_ref.shape, F32)
    kparts, vparts = [], []
    for h in range(nh):
        kn = _dot(c, wuk_ref[:, h * LANE:(h + 1) * LANE])
        ksc = lax.rsqrt(jnp.sum(kn * kn, axis=-1, keepdims=True) * (1.0 / nope) + EPS)
        ks_out = jnp.where(lane == h, ksc, ks_out)
        kparts += [((kn * ksc) * gkn_ref[...]).astype(BF16), krb]
        vparts.append(_dot(c, wuv_ref[:, h * LANE:(h + 1) * LANE]).astype(BF16))
    kcat_ref[...] = jnp.concatenate(kparts, axis=1)
    v_ref[...] = jnp.concatenate(vparts, axis=1)
    ks_ref[...] = ks_out


def kv_up(c, kr_pad, wuk_bf, wuv_bf, g_kn, nh, nope):
    t, kvl = c.shape
    tm = _pick(t, (256, 128, 64, 32, 16))
    row = lambda w: pl.BlockSpec((tm, w), lambda i: (i, 0))
    full = lambda r, w: pl.BlockSpec((r, w), lambda i: (0, 0))
    return pl.pallas_call(
        functools.partial(_kv_up_kernel, nh=nh, nope=nope),
        grid=(t // tm,),
        in_specs=[row(kvl), row(LANE), full(kvl, nh * LANE), full(kvl, nh * LANE), full(1, LANE)],
        out_specs=[row(2 * nh * LANE), row(nh * LANE), row(LANE)],
        out_shape=[jax.ShapeDtypeStruct((t, 2 * nh * LANE), BF16), jax.ShapeDtypeStruct((t, nh * LANE), BF16),
                   jax.ShapeDtypeStruct((t, LANE), F32)],
        compiler_params=_cparams("parallel"),
        name="kv_up",
    )(c, kr_pad, wuk_bf, wuv_bf, g_kn.reshape(1, LANE))


def _flash_kernel(q_ref, k_ref, v_ref, o_ref, *, tq, scale):
    i = pl.program_id(2)
    q = q_ref[...]

    def step(j, carry, diagonal):
        m, l, acc = carry
        off = pl.multiple_of(j * tq, tq)
        s = _nt(q, k_ref[pl.ds(off, tq), :]) * scale
        if diagonal:
            s = jnp.where(lax.broadcasted_iota(jnp.int32, s.shape, 0) >= lax.broadcasted_iota(jnp.int32, s.shape, 1),
                          s, -jnp.inf)
        m_new = jnp.maximum(m, jnp.max(s, axis=-1, keepdims=True))
        alpha = jnp.exp(m - m_new)
        p = jnp.exp(s - m_new)
        l = l * alpha + jnp.sum(p, axis=-1, keepdims=True)
        acc = acc * alpha + _dot(p.astype(BF16), v_ref[pl.ds(off, tq), :])
        return m_new, l, acc

    init = (jnp.full((tq, 1), -jnp.inf, F32), jnp.zeros((tq, 1), F32), jnp.zeros((tq, LANE), F32))
    carry = lax.fori_loop(0, i, lambda j, c: step(j, c, False), init)
    _, l, acc = step(i, carry, True)
    o_ref[...] = (acc / l).astype(o_ref.dtype)


def flash_prompt(qcat, kcat, v, rows_out, bp, sp, nh, scale):
    tq = _pick(sp, (512, 256, 128))
    nq = sp // tq
    return pl.pallas_call(
        functools.partial(_flash_kernel, tq=tq, scale=scale),
        grid=(bp, nh, nq),
        in_specs=[pl.BlockSpec((tq, 2 * LANE), lambda b, h, i: (b * nq + i, h)),
                  pl.BlockSpec((sp, 2 * LANE), lambda b, h, i: (b, h)),
                  pl.BlockSpec((sp, LANE), lambda b, h, i: (b, h))],
        out_specs=pl.BlockSpec((tq, LANE), lambda b, h, i: (b * nq + i, h)),
        out_shape=jax.ShapeDtypeStruct((rows_out, nh * LANE), BF16),
        compiler_params=_cparams("parallel", "parallel", "arbitrary"),
        name="flash_prompt",
    )(qcat, kcat, v)


def _q_absorb_kernel(q_ref, w_ref, g_ref, o_ref):
    qn = (q_ref[:, :LANE] * g_ref[...]).astype(BF16)
    o_ref[...] = _nt(qn, w_ref[...]).astype(o_ref.dtype)


def q_absorb(qs, wuk_bf, g_kn, nh, kvl):
    ts = qs.shape[0]
    return pl.pallas_call(
        _q_absorb_kernel,
        grid=(nh,),
        in_specs=[pl.BlockSpec((ts, 2 * LANE), lambda h: (0, h)),
                  pl.BlockSpec((kvl, LANE), lambda h: (0, h)),
                  pl.BlockSpec((1, LANE), lambda h: (0, 0))],
        out_specs=pl.BlockSpec((ts, kvl), lambda h: (0, h)),
        out_shape=jax.ShapeDtypeStruct((ts, nh * kvl), BF16),
        compiler_params=_cparams("parallel"),
        name="q_absorb",
    )(qs, wuk_bf, g_kn.reshape(1, LANE))


def _o_up_kernel(o_ref, w_ref, out_ref):
    out_ref[...] = _dot(o_ref[...], w_ref[...]).astype(out_ref.dtype)


def o_up(o_lat, wuv_bf, nh, kvl):
    ts = o_lat.shape[0]
    return pl.pallas_call(
        _o_up_kernel,
        grid=(nh,),
        in_specs=[pl.BlockSpec((ts, kvl), lambda h: (0, h)), pl.BlockSpec((kvl, LANE), lambda h: (0, h))],
        out_specs=pl.BlockSpec((ts, LANE), lambda h: (0, h)),
        out_shape=jax.ShapeDtypeStruct((ts, nh * LANE), BF16),
        compiler_params=_cparams("parallel"),
        name="o_up",
    )(o_lat, wuv_bf)


def _decode_kernel(pt_ref, qlat_ref, qr_ref, slat_ref, skr_ref, sks_ref, *rest, pp, nh, tq, scale):
    pages = rest[:3 * pp]
    o_ref = rest[3 * pp]
    m_ref, l_ref, acc_ref = rest[3 * pp + 1:]
    c = pl.program_id(1)
    qlat = qlat_ref[0]
    qr = qr_ref[0]

    def scores(lat_b, kr_t, ks_t):
        ksx = jnp.concatenate([ks_t] * tq, axis=0)
        return (_nt(qlat, lat_b) * ksx + _dot(qr, kr_t.astype(BF16))) * scale

    @pl.when(c == 0)
    def _():
        lat_b = slat_ref[0].astype(BF16)
        s = scores(lat_b, skr_ref[0], sks_ref[0])
        u = lax.broadcasted_iota(jnp.int32, s.shape, 1)
        r = lax.broadcasted_iota(jnp.int32, s.shape, 0)
        s = jnp.where(u * nh <= r, s, -jnp.inf)
        m0 = jnp.max(s, axis=-1, keepdims=True)
        p = jnp.exp(s - m0)
        m_ref[...] = m0
        l_ref[...] = jnp.sum(p, axis=-1, keepdims=True)
        acc_ref[...] = _dot(p.astype(BF16), lat_b)

    grp = DECODE_PAGE_GROUP if pp % DECODE_PAGE_GROUP == 0 else 1
    cat = lambda kind, s0, axis: jnp.concatenate([pages[3 * s + kind][...] for s in range(s0, s0 + grp)], axis=axis)
    lats = [cat(0, s0, 0).astype(BF16) for s0 in range(0, pp, grp)]
    ss = [scores(lats[s0 // grp], cat(1, s0, 1), cat(2, s0, 1)) for s0 in range(0, pp, grp)]
    m_prev = m_ref[...]
    m_new = m_prev
    for s in ss:
        m_new = jnp.maximum(m_new, jnp.max(s, axis=-1, keepdims=True))
    alpha = jnp.exp(m_prev - m_new)
    l = l_ref[...] * alpha
    acc = acc_ref[...] * alpha
    for s, lat_b in zip(ss, lats):
        p = jnp.exp(s - m_new)
        l = l + jnp.sum(p, axis=-1, keepdims=True)
        acc = acc + _dot(p.astype(BF16), lat_b)
    m_ref[...] = m_new
    l_ref[...] = l
    acc_ref[...] = acc

    @pl.when(c == pl.num_programs(1) - 1)
    def _():
        o_ref[0] = (acc / l).astype(o_ref.dtype)


def decode_attn(qlat, qr, self_lat, self_kr_t, self_ks_t, cache_lat, cache_kr_t, cache_ks_t, page_table,
                layer, nh, tq, scale):
    bs, rows, kvl = qlat.shape
    n_pages = page_table.shape[1]
    page = cache_lat.shape[2]
    rope = cache_kr_t.shape[2]
    pp = _pick(n_pages, (16, 8, 4, 2, 1))
    nc = n_pages // pp

    def per_b(shape):
        return pl.BlockSpec((1,) + shape, lambda b, c, pt: (b, 0, 0))

    def paged(shape, s):
        return pl.BlockSpec((None, None) + shape, lambda b, c, pt: (layer, pt[b * n_pages + c * pp + s], 0, 0))

    in_specs = [per_b((rows, kvl)), per_b((rows, rope)), per_b((page, kvl)), per_b((rope, page)), per_b((nh, page))]
    args = [qlat, qr, self_lat, self_kr_t, self_ks_t]
    for s in range(pp):
        in_specs += [paged((page, kvl), s), paged((rope, page), s), paged((nh, page), s)]
        args += [cache_lat, cache_kr_t, cache_ks_t]
    return pl.pallas_call(
        functools.partial(_decode_kernel, pp=pp, nh=nh, tq=tq, scale=scale),
        grid_spec=pltpu.PrefetchScalarGridSpec(
            num_scalar_prefetch=1,
            grid=(bs, nc),
            in_specs=in_specs,
            out_specs=pl.BlockSpec((1, rows, kvl), lambda b, c, pt: (b, 0, 0)),
            scratch_shapes=[pltpu.VMEM((rows, 1), F32), pltpu.VMEM((rows, 1), F32), pltpu.VMEM((rows, kvl), F32)],
        ),
        out_shape=jax.ShapeDtypeStruct((bs, rows, kvl), BF16),
        compiler_params=_cparams("parallel", "arbitrary"),
        name="decode_attn",
    )(page_table.reshape(-1), *args)


def _conv_rows(x, hist, w, bias, kf):
    row = lax.broadcasted_iota(jnp.int32, x.shape, 0)
    y = bias + x * w[kf - 1:kf]
    for s in range(1, kf):
        xs = pltpu.roll(x, s, 0)
        for t in range(s):
            xs = jnp.where(row == t, hist[kf - 1 - s + t:kf - s + t], xs)
        y = y + xs * w[kf - 1 - s:kf - s]
    return y


def _conv_prompt_kernel(*refs, kf, gate):
    if gate:
        ug, uv, hg, hv, wg, wv, bg, bv, o_ref = refs
        a = _conv_rows(ug[...], hg[0], wg[...], bg[...], kf)
        b = _conv_rows(uv[...], hv[0], wv[...], bv[...], kf)
        o_ref[...] = (_silu(a) * b).astype(o_ref.dtype)
    else:
        u, h, w, b, o_ref = refs
        o_ref[...] = _silu(_conv_rows(u[...], h[0], w[...], b[...], kf)).astype(o_ref.dtype)


def _conv_col_blocks(col0, width, gate):
    tc = _pick(math.gcd(width, col0) if col0 else width, (256, 128))
    nj = width // tc
    return tc, nj, col0 // tc, ((0, nj) if gate else (0,))


def conv_prompt(u, hist, w, bias, layer, rows_out, nb, sl, col0, width, gate, out_dtype):
    kf = w.shape[1]
    tc, nj, o0, halves = _conv_col_blocks(col0, width, gate)
    u_specs = [pl.BlockSpec((sl, tc), lambda b, j, o=o: (b, o0 + o + j)) for o in halves]
    h_specs = [pl.BlockSpec((1, kf - 1, tc), lambda b, j, o=o: (b, 0, o + j)) for o in halves]
    w_specs = [pl.BlockSpec((None, kf, tc), lambda b, j, o=o: (layer, 0, o + j)) for o in halves]
    b_specs = [pl.BlockSpec((None, 1, tc), lambda b, j, o=o: (layer, 0, o + j)) for o in halves]
    n_in = len(halves)
    return pl.pallas_call(
        functools.partial(_conv_prompt_kernel, kf=kf, gate=gate),
        grid=(nb, nj),
        in_specs=u_specs + h_specs + w_specs + b_specs,
        out_specs=pl.BlockSpec((sl, tc), lambda b, j: (b, j)),
        out_shape=jax.ShapeDtypeStruct((rows_out, width), out_dtype),
        compiler_params=_cparams("parallel", "parallel"),
        name="conv_prompt_gate" if gate else "conv_prompt",
    )(*([u] * n_in + [hist] * n_in + [w] * n_in + [bias[:, None, :]] * n_in))


def _conv_sample_kernel(*refs, kf, ss, bs, gate):
    def conv(u_ref, h_ref, w_ref, b_ref, t):
        y = b_ref[...]
        for j in range(kf):
            src = h_ref[t + j] if t + j < kf - 1 else u_ref[(t + j - (kf - 1)) * bs:(t + j - (kf - 2)) * bs, :]
            y = y + src * w_ref[j:j + 1, :]
        return y

    if gate:
        ug, uv, hg, hv, wg, wv, bg, bv, o_ref = refs
        for t in range(ss):
            o_ref[t * bs:(t + 1) * bs, :] = (_silu(conv(ug, hg, wg, bg, t)) * conv(uv, hv, wv, bv, t)).astype(o_ref.dtype)
    else:
        u, h, w, b, o_ref = refs
        for t in range(ss):
            o_ref[t * bs:(t + 1) * bs, :] = _silu(conv(u, h, w, b, t)).astype(o_ref.dtype)


def conv_sample(u, hist_t, w, bias, layer, row0, ss, bs, col0, width, gate, out_dtype):
    kf = w.shape[1]
    ts = ss * bs
    assert row0 % ts == 0
    rb = row0 // ts
    tc, nj, o0, halves = _conv_col_blocks(col0, width, gate)
    u_specs = [pl.BlockSpec((ts, tc), lambda j, o=o: (rb, o0 + o + j)) for o in halves]
    h_specs = [pl.BlockSpec((kf - 1, bs, tc), lambda j, o=o: (0, 0, o + j)) for o in halves]
    w_specs = [pl.BlockSpec((None, kf, tc), lambda j, o=o: (layer, 0, o + j)) for o in halves]
    b_specs = [pl.BlockSpec((None, 1, tc), lambda j, o=o: (layer, 0, o + j)) for o in halves]
    n_in = len(halves)
    return pl.pallas_call(
        functools.partial(_conv_sample_kernel, kf=kf, ss=ss, bs=bs, gate=gate),
        grid=(nj,),
        in_specs=u_specs + h_specs + w_specs + b_specs,
        out_specs=pl.BlockSpec((ts, tc), lambda j: (0, j)),
        out_shape=jax.ShapeDtypeStruct((ts, width), out_dtype),
        compiler_params=_cparams("parallel"),
        name="conv_sample_gate" if gate else "conv_sample",
    )(*([u] * n_in + [hist_t] * n_in + [w] * n_in + [bias[:, None, :]] * n_in))


HALO_ROWS = 16
TAIL_ROWS = 8


def _ffn_up_prompt_kernel(a_ref, halo_ref, wg_ref, wv_ref, cwg_ref, cwv_ref, bg_ref, bv_ref,
                          g_ref, tg_ref, tv_ref, *, kf, tiles_per_seq):
    i = pl.program_id(0)
    a = a_ref[...]
    halo = halo_ref[...]
    keep = jnp.where(i % tiles_per_seq == 0, 0.0, 1.0)
    tm = a.shape[0]

    def half(w_ref, cw_ref, b_ref, t_ref):
        w = w_ref[...].astype(BF16)
        u = _dot(a, w)
        hist = _dot(halo, w)[HALO_ROWS - (kf - 1):] * keep
        t_ref[...] = u[tm - TAIL_ROWS:]
        return _conv_rows(u, hist, cw_ref[...], b_ref[...], kf)

    g_ref[...] = (_silu(half(wg_ref, cwg_ref, bg_ref, tg_ref)) * half(wv_ref, cwv_ref, bv_ref, tv_ref)).astype(g_ref.dtype)


def _ffn_up_sample_kernel(a_ref, hg_ref, hv_ref, wg_ref, wv_ref, cwg_ref, cwv_ref, bg_ref, bv_ref,
                          g_ref, tg_ref, tv_ref, *, kf, ss, bs):
    a = a_ref[...]

    def half(w_ref, h_ref, cw_ref, b_ref, t_ref):
        u = _dot(a, w_ref[...].astype(BF16))
        t_ref[...] = u[(ss - (kf - 1)) * bs:]
        ys = []
        for t in range(ss):
            y = b_ref[...]
            for j in range(kf):
                src = h_ref[t + j] if t + j < kf - 1 else u[(t + j - (kf - 1)) * bs:(t + j - (kf - 2)) * bs]
                y = y + src * cw_ref[j:j + 1, :]
            ys.append(y)
        return ys

    yg = half(wg_ref, hg_ref, cwg_ref, bg_ref, tg_ref)
    yv = half(wv_ref, hv_ref, cwv_ref, bv_ref, tv_ref)
    for t in range(ss):
        g_ref[t * bs:(t + 1) * bs, :] = (_silu(yg[t]) * yv[t]).astype(g_ref.dtype)


def ffn_up_gate(h, hist_t, w_up, conv_w, conv_b, layer, bp, sp, ss, bs):
    t_all, d = h.shape
    tp, ts = bp * sp, ss * bs
    dff = w_up.shape[2] // 2
    kf = conv_w.shape[1]
    tc = _pick(dff, (256, 128))
    nj = dff // tc
    tm = _pick(sp, (1024, 512, 256, 128, 64, 32, 16))
    nt = tp // tm
    assert tm % HALO_ROWS == 0 and tp % ts == 0 and kf - 1 <= TAIL_ROWS

    def cols(shape, first_dims, o):
        return pl.BlockSpec(shape, lambda *g, o=o: first_dims + (o + g[-1],))

    def wspecs(o):
        return [cols((None, d, tc), (layer, 0), o)]

    def cspecs(o):
        return [cols((None, kf, tc), (layer, 0), o), cols((None, 1, tc), (layer, 0), o)]

    cb = conv_b[:, None, :]
    g, tg, tv = pl.pallas_call(
        functools.partial(_ffn_up_prompt_kernel, kf=kf, tiles_per_seq=sp // tm),
        grid=(nt, nj),
        in_specs=[pl.BlockSpec((tm, d), lambda i, j: (i, 0), pipeline_mode=pl.Buffered(1)),
                  pl.BlockSpec((HALO_ROWS, d), lambda i, j: (jnp.maximum(i * (tm // HALO_ROWS) - 1, 0), 0))]
                 + wspecs(0) + wspecs(nj) + [cspecs(0)[0], cspecs(nj)[0], cspecs(0)[1], cspecs(nj)[1]],
        out_specs=[pl.BlockSpec((tm, tc), lambda i, j: (i, j)),
                   pl.BlockSpec((TAIL_ROWS, tc), lambda i, j: (i, j)),
                   pl.BlockSpec((TAIL_ROWS, tc), lambda i, j: (i, j))],
        out_shape=[jax.ShapeDtypeStruct((t_all, dff), BF16),
                   jax.ShapeDtypeStruct((nt * TAIL_ROWS, dff), F32),
                   jax.ShapeDtypeStruct((nt * TAIL_ROWS, dff), F32)],
        compiler_params=_cparams("parallel", "arbitrary"),
        name="ffn_up_prompt",
    )(h, h, w_up, w_up, conv_w, conv_w, cb, cb)
    per_seq = lambda r: r.reshape(nt, TAIL_ROWS, dff)[sp // tm - 1::sp // tm, TAIL_ROWS - (kf - 1):]
    tails_p = jnp.concatenate([per_seq(tg), per_seq(tv)], axis=-1)

    hspec = lambda o: [cols((kf - 1, bs, tc), (0, 0), o)]
    g_s, sg, sv = pl.pallas_call(
        functools.partial(_ffn_up_sample_kernel, kf=kf, ss=ss, bs=bs),
        grid=(nj,),
        in_specs=[pl.BlockSpec((ts, d), lambda j: (tp // ts, 0))] + hspec(0) + hspec(nj)
                 + wspecs(0) + wspecs(nj) + [cspecs(0)[0], cspecs(nj)[0], cspecs(0)[1], cspecs(nj)[1]],
        out_specs=[pl.BlockSpec((ts, tc), lambda j: (0, j)),
                   pl.BlockSpec(((kf - 1) * bs, tc), lambda j: (0, j)),
                   pl.BlockSpec(((kf - 1) * bs, tc), lambda j: (0, j))],
        out_shape=[jax.ShapeDtypeStruct((ts, dff), BF16),
                   jax.ShapeDtypeStruct(((kf - 1) * bs, dff), F32),
                   jax.ShapeDtypeStruct(((kf - 1) * bs, dff), F32)],
        compiler_params=_cparams("arbitrary"),
        name="ffn_up_sample",
    )(h, hist_t, hist_t, w_up, w_up, conv_w, conv_w, cb, cb)
    tails_s = jnp.concatenate([sg, sv], axis=-1).reshape(kf - 1, bs, 2 * dff).swapaxes(0, 1)
    return g, g_s, tails_p, tails_s


def _gated_group_norm(y, z, gn):
    y = y * _silu(z)
    ms = jnp.sum(y * y, axis=-1, keepdims=True) * (1.0 / y.shape[-1])
    return (y * lax.rsqrt(ms + EPS)) * gn


def _ssd_prompt_kernel(xs_ref, bm_ref, cm_ref, z_ref, dt_ref, dtt_ref, bias_ref, alog_ref, dskip_ref, gn_ref,
                       biast_ref, alogt_ref, y_ref, hl_ref, h_ref, *, hp, hg):
    g = pl.program_id(1)
    c = pl.program_id(2)

    @pl.when(c == 0)
    def _():
        h_ref[...] = jnp.zeros_like(h_ref)

    xs = xs_ref[...]
    L, gw = xs.shape
    nh = dt_ref.shape[1]
    hpl = LANE // hp
    bm = bm_ref[...].astype(BF16)
    cm = cm_ref[...].astype(BF16)
    ri = lax.broadcasted_iota(jnp.int32, (L, L), 0)
    ci = lax.broadcasted_iota(jnp.int32, (L, L), 1)
    causal = ri >= ci
    dtc = _softplus(dt_ref[...] + bias_ref[...])
    tril = causal.astype(BF16)
    cumc = sum(_dot(tril, piece) for piece in _split3(dtc * (-jnp.exp(alog_ref[...]))))
    spread = (lax.broadcasted_iota(jnp.int32, (nh, gw), 0)
              == g * hg + lax.broadcasted_iota(jnp.int32, (nh, gw), 1) // hp).astype(BF16)

    def to_lanes(v):
        return sum(_dot(piece, spread) for piece in _split3(v))

    dt = to_lanes(dtc)
    cum = to_lanes(cumc)
    dtt = _softplus(dtt_ref[...] + biast_ref[...])
    triu = (ri <= ci).astype(BF16)
    cumt = sum(_dot(piece, triu) for piece in _split3(dtt * (-jnp.exp(alogt_ref[...]))))
    xdt = xs * dt
    cb = _nt(cm, bm)
    cum_last = cum[L - 1:L, :]
    xw = xdt * jnp.exp(cum_last - cum)
    lane_head = lax.broadcasted_iota(jnp.int32, (L, LANE), 1) // hp
    row_head = lax.broadcasted_iota(jnp.int32, (LANE, 1), 0) // hp
    h_all = h_ref[...]
    ys, xwts, dcols = [], [], []
    for pi in range(gw // LANE):
        sl = slice(pi * LANE, (pi + 1) * LANE)
        xdt_p = xdt[:, sl]
        ms, xks = [], []
        dcol = jnp.zeros((LANE, 1), F32)
        for q in range(hpl):
            k = pi * hpl + q
            colb = jnp.broadcast_to(cum[:, k * hp:k * hp + 1], (L, L))
            rowb = jnp.broadcast_to(cumt[k:k + 1, :], (L, L))
            dec = jnp.where(causal, jnp.exp(colb - rowb), 0.0)
            ms.append((cb * dec).astype(BF16))
            xks.append(jnp.where(lane_head == q, xdt_p, 0.0).astype(BF16))
            dcol = jnp.where(row_head == q, jnp.exp(cumt[k:k + 1, L - 1:L]), dcol)
        ys.append(_dot(jnp.concatenate(ms, axis=1), jnp.concatenate(xks, axis=0)))
        xwts.append(xw[:, sl].T)
        dcols.append(dcol)
    y = jnp.concatenate(ys, axis=1) + _nt(cm, h_all.astype(BF16)) * jnp.exp(cum) + xs * dskip_ref[...]
    xwt = jnp.concatenate(xwts, axis=0).astype(BF16)
    h_ref[...] = h_all * jnp.concatenate(dcols, axis=0) + _dot(xwt, bm)
    y_ref[...] = _gated_group_norm(y, z_ref[...], gn_ref[...]).astype(y_ref.dtype)

    @pl.when(c == pl.num_programs(2) - 1)
    def _():
        hl_ref[0] = h_ref[...]


def ssd_prompt(xbc, proj, dt_t, dt_bias, a_log, dskip_rep, gnorm, layer, rows_out, bp, sp, nh, hp, ns, ng):
    L = min(SSM_CHUNK, sp)
    nch = sp // L
    d_inner = nh * hp
    gw = d_inner // ng
    hg = nh // ng
    b0 = d_inner // ns
    dtb = (proj.shape[1] - nh) // nh
    assert nh % LANE == 0 and (proj.shape[1] - nh) % nh == 0
    rowb = lambda w, f: pl.BlockSpec((L, w), f)
    lvec = lambda w: pl.BlockSpec((None, 1, w), lambda b, g, c: (layer, 0, 0))
    gvec = pl.BlockSpec((None, 1, gw), lambda b, g, c: (layer, 0, g))
    vect = pl.BlockSpec((None, hg, 1), lambda b, g, c: (layer, g, 0))
    return pl.pallas_call(
        functools.partial(_ssd_prompt_kernel, hp=hp, hg=hg),
        grid=(bp, ng, nch),
        in_specs=[rowb(gw, lambda b, g, c: (b * nch + c, g)),
                  rowb(ns, lambda b, g, c: (b * nch + c, b0 + g)),
                  rowb(ns, lambda b, g, c: (b * nch + c, b0 + ng + g)),
                  rowb(gw, lambda b, g, c: (b * nch + c, g)),
                  rowb(nh, lambda b, g, c: (b * nch + c, dtb)),
                  pl.BlockSpec((hg, L), lambda b, g, c: (g, b * nch + c)),
                  lvec(nh), lvec(nh), gvec, gvec, vect, vect],
        out_specs=[rowb(gw, lambda b, g, c: (b * nch + c, g)),
                   pl.BlockSpec((1, gw, ns), lambda b, g, c: (b, g, 0))],
        out_shape=[jax.ShapeDtypeStruct((rows_out, d_inner), BF16),
                   jax.ShapeDtypeStruct((bp, d_inner, ns), F32)],
        scratch_shapes=[pltpu.VMEM((gw, ns), F32)],
        compiler_params=_cparams("parallel", "parallel", "arbitrary"),
        name="ssd_prompt",
    )(xbc, xbc, xbc, proj, proj, dt_t, dt_bias[:, None, :], a_log[:, None, :], dskip_rep, gnorm,
      dt_bias[:, :, None], a_log[:, :, None])


def _ssd_sample_kernel(*refs, ss, hp, ng, chained):
    if chained:
        refs = refs[1:]
    xs_ref, bm_ref, cm_ref, z_ref, dt_ref, bias_ref, alog_ref, dskip_ref, gn_ref, h0_ref, y_ref, hn_ref = refs
    lp = xs_ref.shape[1]
    gw = xs_ref.shape[2] // ng
    ns = bm_ref.shape[2] // ng
    hpl = LANE // hp
    row = lax.broadcasted_iota(jnp.int32, (lp, gw), 0)
    row_head = lax.broadcasted_iota(jnp.int32, (LANE, 1), 0) // hp

    def group(g, carry):
        cg = pl.ds(pl.multiple_of(g * gw, gw), gw)
        cn = pl.ds(pl.multiple_of(g * ns, ns), ns)
        xs = xs_ref[0, :, cg]
        bm = bm_ref[0, :, cn]
        cm = cm_ref[0, :, cn]
        dt = jnp.where(row < ss, _softplus(dt_ref[0, :, cg] + bias_ref[:, cg]), 0.0)
        dta = dt * (-jnp.exp(alog_ref[:, cg]))
        cums = [dta[0:1]]
        for l in range(1, ss):
            cums.append(cums[-1] + dta[l:l + 1])
        xdt = xs * dt
        h0 = h0_ref[0, cg, :]
        yoff = _nt(cm.astype(BF16), h0.astype(BF16))
        y = jnp.zeros((lp, gw), F32)
        cum_full = jnp.broadcast_to(cums[ss - 1], (lp, gw))
        for l in range(ss):
            acc = yoff[l:l + 1] * jnp.exp(cums[l]) + xs[l:l + 1] * dskip_ref[:, cg]
            for s in range(l + 1):
                cb = jnp.sum(cm[l:l + 1] * bm[s:s + 1], axis=-1, keepdims=True)
                acc = acc + (cb * jnp.exp(cums[l] - cums[s])) * xdt[s:s + 1]
            y = jnp.where(row == l, acc, y)
            cum_full = jnp.where(row == l, cums[l], cum_full)
        y_ref[0, :, cg] = _gated_group_norm(y, z_ref[0, :, cg], gn_ref[:, cg]).astype(y_ref.dtype)

        xw = xdt * jnp.exp(cums[ss - 1] - cum_full)
        xw_pad = jnp.concatenate([xw, jnp.zeros((LANE - lp, gw), F32)], axis=0)
        bm_pad = jnp.concatenate([bm, jnp.zeros((LANE - lp, ns), F32)], axis=0).astype(BF16)
        dlast = jnp.exp(cums[ss - 1])
        xwts, dcols = [], []
        for pi in range(gw // LANE):
            sl = slice(pi * LANE, (pi + 1) * LANE)
            dcol = jnp.zeros((LANE, 1), F32)
            for q in range(hpl):
                k = pi * hpl + q
                dcol = jnp.where(row_head == q, dlast[0:1, k * hp:k * hp + 1], dcol)
            dcols.append(dcol)
            xwts.append(xw_pad[:, sl].T)
        xwt = jnp.concatenate(xwts, axis=0).astype(BF16)
        hn_ref[0, cg, :] = h0 * jnp.concatenate(dcols, axis=0) + _dot(xwt, bm_pad)
        return carry

    lax.fori_loop(0, ng, group, 0)


def ssd_sample(xs, bm, cm, z, dt_rep, bias_rep, alog_rep, dskip_rep, gnorm, state, state_out, layer, ss, hp, ng):
    bs, lp, d_inner = xs.shape
    ns = state.shape[-1]
    act = lambda w: pl.BlockSpec((1, lp, w), lambda b: (b, 0, 0))
    vec = pl.BlockSpec((None, 1, d_inner), lambda b: (layer, 0, 0))
    st = pl.BlockSpec((None, 1, d_inner, ns), lambda b: (layer, b, 0, 0))
    chained = state_out is not None
    in_specs = [act(d_inner), act(ng * ns), act(ng * ns), act(d_inner), act(d_inner), vec, vec, vec, vec, st]
    args = [xs, bm, cm, z, dt_rep, bias_rep, alog_rep, dskip_rep, gnorm, state]
    if chained:
        in_specs = [pl.BlockSpec(memory_space=pl.ANY)] + in_specs
        args = [state_out] + args
    return pl.pallas_call(
        functools.partial(_ssd_sample_kernel, ss=ss, hp=hp, ng=ng, chained=chained),
        grid=(bs,),
        in_specs=in_specs,
        out_specs=[act(d_inner), st],
        out_shape=[jax.ShapeDtypeStruct((bs, lp, d_inner), BF16), jax.ShapeDtypeStruct(state.shape, F32)],
        input_output_aliases={0: 1} if chained else {},
        compiler_params=_cparams("parallel"),
        name="ssd_sample",
    )(*args)


def kernel(x_prompt, x_sample, cache_mla_latent, cache_mla_krope, cache_mla_kscale, page_table, state_ssm, state_ssm_conv, state_ffn_conv, norm_mix, norm_ffn, w_mla_in, g_q_lora, g_kv_lora, w_uq, w_uk, w_uv, w_mla_out, g_q_nope, g_q_rope, g_k_nope, g_k_rope, w_ssm_in, conv_w_ssm, conv_b_ssm, dt_bias, a_log, d_skip, g_ssm_norm, w_ssm_out, w_ffn_up, conv_w_ffn, conv_b_ffn, w_ffn_down):
    bp, sp, d = x_prompt.shape
    bs, ss, _ = x_sample.shape
    tp, ts = bp * sp, bs * ss
    t_all = tp + ts
    depth = norm_mix.shape[0]
    kvl, nh, nope = w_uk.shape[1:]
    vd = w_uv.shape[3]
    ql = g_q_lora.shape[1]
    rope = g_q_rope.shape[1]
    qk = nope + rope
    assert nope == LANE and vd == LANE and 2 * rope == LANE
    n_pages = page_table.shape[1]
    page = cache_mla_latent.shape[2]
    past_len = n_pages * page
    scale = 1.0 / math.sqrt(qk)
    n_ssm, _, nsh, hp, ns = state_ssm.shape
    d_inner = nsh * hp
    conv_dim = conv_w_ssm.shape[2]
    ng = (conv_dim - d_inner) // (2 * ns)
    kc = conv_w_ssm.shape[1]
    dff = w_ffn_down.shape[1]
    kf = conv_w_ffn.shape[1]
    assert ss >= kc - 1 and ss >= kf - 1 and ss <= SAMPLE_ROWS_PAD and LANE % hp == 0 and ns == LANE
    lp = SAMPLE_ROWS_PAD

    to_bt = lambda r: r.reshape((ss, bs) + r.shape[1:]).swapaxes(0, 1)
    to_tb = lambda r: r.swapaxes(0, 1).reshape((ts,) + r.shape[2:])
    put_sample = lambda buf, rows: lax.dynamic_update_slice(buf, rows, (tp, 0))
    seq_tails = lambda r, n, c0, c1: jnp.stack([r[(b + 1) * sp - n:(b + 1) * sp, c0:c1] for b in range(bp)])

    half = rope // 2
    inv = jnp.exp(-math.log(ROPE_THETA) * jnp.arange(half, dtype=F32) / half)
    pos = jnp.concatenate([jnp.tile(jnp.arange(sp), bp), jnp.repeat(past_len + jnp.arange(ss), bs)])
    ang = pos.astype(F32)[:, None] * inv[None, :]
    cos, sin = jnp.cos(ang), jnp.sin(ang)
    zt = jnp.zeros_like(cos)
    tabs = (jnp.concatenate([cos, cos, zt, zt], 1), jnp.concatenate([-sin, zt, zt, zt], 1),
            jnp.concatenate([zt, sin, zt, zt], 1))
    tabs_s = tuple(t[tp:] for t in tabs)

    def pad_lane(g):
        return jnp.pad(g, (0, LANE - g.shape[0])).reshape(1, LANE)

    cache_kr_t = jnp.swapaxes(cache_mla_krope, 2, 3)
    cache_ks_t = jnp.swapaxes(cache_mla_kscale, 2, 3)
    state4 = state_ssm.reshape(n_ssm, bs, d_inner, ns)
    dskip_rep = jnp.repeat(d_skip, hp, axis=1)[:, None, :]
    bias_rep = jnp.repeat(dt_bias, hp, axis=1)[:, None, :]
    alog_rep = jnp.repeat(a_log, hp, axis=1)[:, None, :]
    gnorm = g_ssm_norm[:, None, :]

    x = jnp.concatenate([x_prompt.reshape(tp, d), to_tb(x_sample)], axis=0)
    lat_o, kr_o, ksc_o, ssm_p_o, sconv_o, fconv_o = [], [], [], [], [], []
    state_new = None

    for i in range(depth):
        j = i // 2
        h = rmsnorm(x, norm_mix[i])
        if i % 2 == 0:
            w_in = jnp.pad(w_mla_in[j], ((0, 0), (0, (-w_mla_in.shape[2]) % 512)))
            a = matmul(h, w_in, name="mla_in")
            qa, c, krp = mla_post_a(a, g_q_lora[j], g_kv_lora[j], pad_lane(g_k_rope[j]), tabs, ql, kvl, rope)
            w_uq_pad = jnp.pad(w_uq[j].reshape(ql, nh, qk), ((0, 0), (0, 0), (0, 2 * LANE - qk))).reshape(ql, nh * 2 * LANE)
            q_raw = matmul(qa, w_uq_pad, name="mla_uq")
            g_qr_pad = pad_lane(g_q_rope[j])
            qcat = q_post(q_raw, g_q_nope[j], g_qr_pad, tabs, nope, rope, BF16)
            wuk_bf = w_uk[j].reshape(kvl, nh * nope).astype(BF16)
            wuv_bf = w_uv[j].reshape(kvl, nh * vd).astype(BF16)
            kcat, v, ksc = kv_up(c, krp, wuk_bf, wuv_bf, g_k_nope[j], nh, nope)
            o = flash_prompt(qcat, kcat, v, t_all, bp, sp, nh, scale)
            qs = q_post(q_raw[tp:], g_q_nope[j], g_qr_pad, tabs_s, nope, rope, F32)
            qlat = to_bt(q_absorb(qs, wuk_bf, g_k_nope[j], nh, kvl)).reshape(bs, ss * nh, kvl)
            qr_s = to_bt(qs).reshape(bs, ss, nh, 2 * LANE)[..., LANE:LANE + rope].reshape(bs, ss * nh, rope).astype(BF16)
            pad_keys = lambda r: jnp.pad(to_bt(r), ((0, 0), (0, page - ss), (0, 0)))
            o_lat = decode_attn(qlat, qr_s, pad_keys(c[tp:]), pad_keys(krp[tp:, :rope]).swapaxes(1, 2),
                                pad_keys(ksc[tp:, :nh]).swapaxes(1, 2), cache_mla_latent, cache_kr_t, cache_ks_t,
                                page_table, j, nh, ss, scale)
            o_s = o_up(to_tb(o_lat.reshape(bs, ss, nh * kvl)), wuv_bf, nh, kvl)
            x = matmul(put_sample(o, o_s), w_mla_out, layer=j, res=x, name="mla_out")
            lat_o.append(c)
            kr_o.append(krp[:, :rope])
            ksc_o.append(ksc[:, :nh])
        else:
            proj = matmul(h, w_ssm_in, layer=j, name="ssm_in")
            hist0 = jnp.zeros((bp, kc - 1, conv_dim), F32)
            xbc_p = conv_prompt(proj, hist0, conv_w_ssm, conv_b_ssm, j, tp, bp, sp, d_inner, conv_dim, False, F32)
            y, hl_p = ssd_prompt(xbc_p, proj, proj[:tp, d_inner + conv_dim:].T, dt_bias, a_log, dskip_rep, gnorm,
                                 j, t_all, bp, sp, nsh, hp, ns, ng)
            xbc_s = conv_sample(proj, state_ssm_conv[j].swapaxes(0, 1), conv_w_ssm, conv_b_ssm, j, tp, ss, bs,
                                d_inner, conv_dim, False, F32)
            proj_s = proj[tp:]
            padr = lambda r: jnp.pad(to_bt(r), ((0, 0), (0, lp - ss), (0, 0)))
            y_s, state_new = ssd_sample(
                padr(xbc_s[:, :d_inner]), padr(xbc_s[:, d_inner:d_inner + ng * ns]), padr(xbc_s[:, d_inner + ng * ns:]),
                padr(proj_s[:, :d_inner]), padr(jnp.repeat(proj_s[:, d_inner + conv_dim:], hp, axis=1)),
                bias_rep, alog_rep, dskip_rep, gnorm, state4, state_new, j, ss, hp, ng)
            x = matmul(put_sample(y, to_tb(y_s[:, :ss])), w_ssm_out, layer=j, res=x, name="ssm_out")
            ssm_p_o.append(hl_p.reshape(bp, nsh, hp, ns))
            sconv_o.append((seq_tails(proj, kc - 1, d_inner, d_inner + conv_dim),
                            to_bt(proj_s[:, d_inner:d_inner + conv_dim])[:, ss - (kc - 1):]))
        h = rmsnorm(x, norm_ffn[i])
        g, g_s, tails_p, tails_s = ffn_up_gate(h, state_ffn_conv[i].swapaxes(0, 1), w_ffn_up, conv_w_ffn, conv_b_ffn,
                                               i, bp, sp, ss, bs)
        x = matmul(put_sample(g, g_s), w_ffn_down, layer=i, res=x, name="ffn_down")
        fconv_o.append((tails_p, tails_s))

    def split(rows, tail):
        return rows[:tp].reshape((bp, sp) + tail), to_bt(rows[tp:])

    y_p, y_s = split(x, (d,))
    lat = [split(c, (kvl,)) for c in lat_o]
    kr = [split(c, (rope,)) for c in kr_o]
    ksc = [split(c, (nh,)) for c in ksc_o]
    stack = lambda pairs, k: jnp.stack([p[k] for p in pairs])
    return (y_p, y_s,
            stack(lat, 0), stack(kr, 0), stack(ksc, 0), jnp.stack(ssm_p_o), stack(sconv_o, 0), stack(fconv_o, 0),
            stack(lat, 1), stack(kr, 1), stack(ksc, 1), state_new.reshape(state_ssm.shape), stack(sconv_o, 1),
            stack(fconv_o, 1))
```

```python
import functools
import math

import jax
import jax.numpy as jnp
from jax import lax
from jax.experimental import pallas as pl
from jax.experimental.pallas import tpu as pltpu

F32, BF16 = jnp.float32, jnp.bfloat16
EPS = 1e-6
LOG2E = 1.4426950408889634
ROPE_THETA = 10000.0
SSM_CHUNK = 128
LANE = 128
VMEM_LIMIT = 56 * 1024 * 1024
MM_VMEM_BUDGET = 52 * 1024 * 1024
MM_TK = 1024
MM_MIN_TN = 256
SAMPLE_ROWS_PAD = 16
DECODE_PAGE_GROUP = 2


def _cparams(*sem):
    return pltpu.CompilerParams(dimension_semantics=sem, vmem_limit_bytes=VMEM_LIMIT)


def _pick(n, cands):
    for c in cands:
        if n % c == 0:
            return c
    return n


def _nt(a, b):
    return lax.dot_general(a, b, (((1,), (1,)), ((), ())), preferred_element_type=F32)


def _dot(a, b):
    return jnp.dot(a, b, preferred_element_type=F32)


def _dot_f32(a, b):
    return jnp.dot(a, b, precision=lax.Precision.HIGHEST, preferred_element_type=F32)


def _split3(x):
    hi = x.astype(BF16)
    r1 = x - hi.astype(F32)
    mid = r1.astype(BF16)
    lo = (r1 - mid.astype(F32)).astype(BF16)
    return hi, mid, lo


def _rms(x, g, n):
    ms = jnp.sum(x * x, axis=-1, keepdims=True) * (1.0 / n)
    return (x * lax.rsqrt(ms + EPS)) * g


def _softplus(x):
    return jnp.maximum(x, 0.0) + jnp.log1p(jnp.exp(-jnp.abs(x)))


def _silu(x):
    return x * (1.0 / (1.0 + jnp.exp(-x)))


def _rope128(r, c, s1, s2, half):
    return r * c + pltpu.roll(r, LANE - half, 1) * s1 + pltpu.roll(r, half, 1) * s2


def _rmsnorm_kernel(x_ref, g_ref, o_ref):
    x = x_ref[...]
    o_ref[...] = _rms(x, g_ref[...], x.shape[-1]).astype(o_ref.dtype)


def rmsnorm(x, g, out_dtype=BF16):
    m, d = x.shape
    tm = _pick(m, (256, 128, 64, 32, 16))
    return pl.pallas_call(
        _rmsnorm_kernel,
        grid=(m // tm,),
        in_specs=[pl.BlockSpec((tm, d), lambda i: (i, 0)), pl.BlockSpec((1, d), lambda i: (0, 0))],
        out_specs=pl.BlockSpec((tm, d), lambda i: (i, 0)),
        out_shape=jax.ShapeDtypeStruct((m, d), out_dtype),
        compiler_params=_cparams("parallel"),
        name="rmsnorm",
    )(x, g.reshape(1, d))


def _mm_full_kernel(*refs, has_res):
    if has_res:
        a_ref, w_ref, r_ref, o_ref = refs
    else:
        a_ref, w_ref, o_ref = refs
    r = _dot(a_ref[...].astype(BF16), w_ref[...].astype(BF16))
    if has_res:
        r = r + r_ref[...]
    o_ref[...] = r.astype(o_ref.dtype)


def _mm_ktiled_kernel(*refs, nk, k_rem, has_res):
    if has_res:
        a_ref, w_ref, r_ref, o_ref, acc_ref = refs
    else:
        a_ref, w_ref, o_ref, acc_ref = refs
    k = pl.program_id(2)
    a = a_ref[...]
    w = w_ref[...]
    if k_rem:
        limit = jnp.where(k == nk - 1, k_rem, a.shape[1])
        a = jnp.where(lax.broadcasted_iota(jnp.int32, a.shape, 1) < limit, a, jnp.zeros_like(a))
        w = jnp.where(lax.broadcasted_iota(jnp.int32, w.shape, 0) < limit, w, jnp.zeros_like(w))
    p = _dot(a.astype(BF16), w.astype(BF16))

    @pl.when(k == 0)
    def _():
        acc_ref[...] = p

    @pl.when(k > 0)
    def _():
        acc_ref[...] += p

    @pl.when(k == nk - 1)
    def _():
        r = acc_ref[...]
        if has_res:
            r = r + r_ref[...]
        o_ref[...] = r.astype(o_ref.dtype)


def _mm_plan(m, k, n, a_bytes, has_res):
    tm = _pick(m, (1088, 1024, 512, 256, 128, 64, 32, 16))
    io_tiles = 2 + 2 * has_res
    for tn in (1024, 640, 512, 384, 256):
        full = tm * k * a_bytes + 2 * k * tn * 4 + io_tiles * tm * tn * 4 + k * tn * 2 + tm * tn * 4
        if n % tn == 0 and tn >= MM_MIN_TN and full <= MM_VMEM_BUDGET:
            return tm, tn, k
    return tm, _pick(n, (1024, 512, 256, 128)), min(k, MM_TK)


def matmul(a, w, layer=None, res=None, out_dtype=F32, name="matmul"):
    m, k = a.shape
    n = w.shape[-1]
    assert w.shape[-2] == k
    tm, tn, tk = _mm_plan(m, k, n, a.dtype.itemsize, res is not None)
    if tk != k and k % (2 * LANE) == 0 and _mm_plan(m, k // 2, n, a.dtype.itemsize, True)[2] == k // 2:
        part = _matmul_call(a, w, layer, res, F32, name, k // 2, 0)
        return _matmul_call(a, w, layer, part, out_dtype, name, k // 2, 1)
    return _matmul_call(a, w, layer, res, out_dtype, name, k, 0)


def _matmul_call(a, w, layer, res, out_dtype, name, k, kpart):
    m = a.shape[0]
    n = w.shape[-1]
    tm, tn, tk = _mm_plan(m, k, n, a.dtype.itemsize, res is not None)
    has_res = res is not None
    if tk == k:
        amap, wmap, omap = (lambda i, j: (i, kpart)), (lambda i, j: (kpart, j)), (lambda i, j: (i, j))
        grid, sem, scratch = (m // tm, n // tn), ("parallel", "arbitrary"), []
        body = functools.partial(_mm_full_kernel, has_res=has_res)
        a_spec = pl.BlockSpec((tm, k), amap, pipeline_mode=pl.Buffered(1))
    else:
        assert kpart == 0
        amap, wmap, omap = (lambda i, j, kk: (i, kk)), (lambda i, j, kk: (kk, j)), (lambda i, j, kk: (i, j))
        nk = pl.cdiv(k, tk)
        grid, sem, scratch = (m // tm, n // tn, nk), ("parallel", "parallel", "arbitrary"), [pltpu.VMEM((tm, tn), F32)]
        body = functools.partial(_mm_ktiled_kernel, nk=nk, k_rem=k % tk, has_res=has_res)
        a_spec = pl.BlockSpec((tm, tk), amap)
    if w.ndim == 3:
        w_spec = pl.BlockSpec((None, tk, tn), lambda *g: (layer,) + wmap(*g))
    else:
        w_spec = pl.BlockSpec((tk, tn), wmap)
    in_specs, args = [a_spec, w_spec], [a, w]
    if has_res:
        in_specs.append(pl.BlockSpec((tm, tn), omap))
        args.append(res)
    return pl.pallas_call(
        body,
        grid=grid,
        in_specs=in_specs,
        out_specs=pl.BlockSpec((tm, tn), omap),
        out_shape=jax.ShapeDtypeStruct((m, n), out_dtype),
        scratch_shapes=scratch,
        compiler_params=_cparams(*sem),
        name=name,
    )(*args)


def _mla_post_a_kernel(a_ref, gq_ref, gkv_ref, gkr_ref, c_ref, s1_ref, s2_ref,
                       qa_ref, lat_ref, kr_ref, *, ql, kvl, rope):
    qa_ref[...] = _rms(a_ref[:, :ql], gq_ref[...], ql).astype(qa_ref.dtype)
    lat_ref[...] = _rms(a_ref[:, ql:ql + kvl], gkv_ref[...], kvl)
    r = _rms(a_ref[:, ql + kvl:ql + kvl + LANE], gkr_ref[...], rope)
    kr_ref[...] = _rope128(r, c_ref[...], s1_ref[...], s2_ref[...], rope // 2)


def mla_post_a(a, g_qa, g_kva, g_kr_pad, tabs, ql, kvl, rope):
    t = a.shape[0]
    tm = _pick(t, (256, 128, 64, 32, 16))
    row = lambda w: pl.BlockSpec((tm, w), lambda i: (i, 0))
    vec = lambda w: pl.BlockSpec((1, w), lambda i: (0, 0))
    return pl.pallas_call(
        functools.partial(_mla_post_a_kernel, ql=ql, kvl=kvl, rope=rope),
        grid=(t // tm,),
        in_specs=[row(a.shape[1]), vec(ql), vec(kvl), vec(LANE), row(LANE), row(LANE), row(LANE)],
        out_specs=[row(ql), row(kvl), row(LANE)],
        out_shape=[jax.ShapeDtypeStruct((t, ql), BF16), jax.ShapeDtypeStruct((t, kvl), F32),
                   jax.ShapeDtypeStruct((t, LANE), F32)],
        compiler_params=_cparams("parallel"),
        name="mla_post_a",
    )(a, g_qa.reshape(1, ql), g_kva.reshape(1, kvl), g_kr_pad, *tabs)


def _q_post_kernel(q_ref, gn_ref, gr_ref, c_ref, s1_ref, s2_ref, o_ref, *, hb, nope, rope):
    c, s1, s2 = c_ref[...], s1_ref[...], s2_ref[...]
    parts = []
    for h in range(hb):
        lo = h * 2 * LANE
        parts.append(_rms(q_ref[:, lo:lo + LANE], gn_ref[...], nope).astype(o_ref.dtype))
        r = _rms(q_ref[:, lo + LANE:lo + 2 * LANE], gr_ref[...], rope)
        parts.append(_rope128(r, c, s1, s2, rope // 2).astype(o_ref.dtype))
    o_ref[...] = jnp.concatenate(parts, axis=1)


def q_post(q_raw, g_qn, g_qr_pad, tabs, nope, rope, out_dtype):
    t, w = q_raw.shape
    tm = _pick(t, (256, 128, 64, 32, 16))
    hb = _pick(w // (2 * LANE), (4, 2, 1))
    bw = hb * 2 * LANE
    tab = pl.BlockSpec((tm, LANE), lambda i, j: (i, 0))
    vec = pl.BlockSpec((1, LANE), lambda i, j: (0, 0))
    return pl.pallas_call(
        functools.partial(_q_post_kernel, hb=hb, nope=nope, rope=rope),
        grid=(t // tm, w // bw),
        in_specs=[pl.BlockSpec((tm, bw), lambda i, j: (i, j)), vec, vec, tab, tab, tab],
        out_specs=pl.BlockSpec((tm, bw), lambda i, j: (i, j)),
        out_shape=jax.ShapeDtypeStruct((t, w), out_dtype),
        compiler_params=_cparams("parallel", "parallel"),
        name="q_post",
    )(q_raw, g_qn.reshape(1, LANE), g_qr_pad, *tabs)


def _kv_up_kernel(c_ref, kr_ref, wuk_ref, wuv_ref, gkn_ref, kcat_ref, v_ref, ks_ref, *, nh, nope):
    c = c_ref[...].astype(BF16)
    krb = kr_ref[...].astype(BF16)
    lane = lax.broadcasted_iota(jnp.int32, ks_ref.shape, 1)
    ks_out = jnp.zeros(ks_ref.shape, F32)
    kparts, vparts = [], []
    for h in range(nh):
        kn = _dot(c, wuk_ref[:, h * LANE:(h + 1) * LANE])
        ksc = lax.rsqrt(jnp.sum(kn * kn, axis=-1, keepdims=True) * (1.0 / nope) + EPS)
        ks_out = jnp.where(lane == h, ksc, ks_out)
        kparts += [((kn * ksc) * gkn_ref[...]).astype(BF16), krb]
        vparts.append(_dot(c, wuv_ref[:, h * LANE:(h + 1) * LANE]).astype(BF16))
    kcat_ref[...] = jnp.concatenate(kparts, axis=1)
    v_ref[...] = jnp.concatenate(vparts, axis=1)
    ks_ref[...] = ks_out


def kv_up(c, kr_pad, wuk_bf, wuv_bf, g_kn, nh, nope):
    t, kvl = c.shape
    tm = _pick(t, (256, 128, 64, 32, 16))
    row = lambda w: pl.BlockSpec((tm, w), lambda i: (i, 0))
    full = lambda r, w: pl.BlockSpec((r, w), lambda i: (0, 0))
    return pl.pallas_call(
        functools.partial(_kv_up_kernel, nh=nh, nope=nope),
        grid=(t // tm,),
        in_specs=[row(kvl), row(LANE), full(kvl, nh * LANE), full(kvl, nh * LANE), full(1, LANE)],
        out_specs=[row(2 * nh * LANE), row(nh * LANE), row(LANE)],
        out_shape=[jax.ShapeDtypeStruct((t, 2 * nh * LANE), BF16), jax.ShapeDtypeStruct((t, nh * LANE), BF16),
                   jax.ShapeDtypeStruct((t, LANE), F32)],
        compiler_params=_cparams("parallel"),
        name="kv_up",
    )(c, kr_pad, wuk_bf, wuv_bf, g_kn.reshape(1, LANE))


def _flash_kernel(q_ref, k_ref, v_ref, o_ref, *, tq, scale):
    i = pl.program_id(2)
    q = q_ref[...]

    def step(j, carry, diagonal):
        m, l, acc = carry
        off = pl.multiple_of(j * tq, tq)
        s = _nt(q, k_ref[pl.ds(off, tq), :])
        if diagonal:
            s = jnp.where(lax.broadcasted_iota(jnp.int32, s.shape, 0) >= lax.broadcasted_iota(jnp.int32, s.shape, 1),
                          s, -jnp.inf)
        m_new = jnp.maximum(m, jnp.max(s, axis=-1, keepdims=True))
        alpha = jnp.exp2((m - m_new) * (scale * LOG2E))
        p = jnp.exp2((s - m_new) * (scale * LOG2E))
        l = l * alpha + jnp.sum(p, axis=-1, keepdims=True)
        acc = acc * alpha + _dot(p.astype(BF16), v_ref[pl.ds(off, tq), :])
        return m_new, l, acc

    init = (jnp.full((tq, 1), -jnp.inf, F32), jnp.zeros((tq, 1), F32), jnp.zeros((tq, LANE), F32))
    carry = lax.fori_loop(0, i, lambda j, c: step(j, c, False), init)
    _, l, acc = step(i, carry, True)
    o_ref[...] = (acc / l).astype(o_ref.dtype)


def flash_prompt(qcat, kcat, v, rows_out, bp, sp, nh, scale):
    tq = _pick(sp, (512, 256, 128))
    nq = sp // tq
    return pl.pallas_call(
        functools.partial(_flash_kernel, tq=tq, scale=scale),
        grid=(bp, nh, nq),
        in_specs=[pl.BlockSpec((tq, 2 * LANE), lambda b, h, i: (b * nq + i, h)),
                  pl.BlockSpec((sp, 2 * LANE), lambda b, h, i: (b, h)),
                  pl.BlockSpec((sp, LANE), lambda b, h, i: (b, h))],
        out_specs=pl.BlockSpec((tq, LANE), lambda b, h, i: (b * nq + i, h)),
        out_shape=jax.ShapeDtypeStruct((rows_out, nh * LANE), BF16),
        compiler_params=_cparams("parallel", "parallel", "arbitrary"),
        name="flash_prompt",
    )(qcat, kcat, v)


def _q_absorb_kernel(q_ref, w_ref, g_ref, o_ref):
    qn = (q_ref[:, :LANE] * g_ref[...]).astype(BF16)
    o_ref[...] = _nt(qn, w_ref[...]).astype(o_ref.dtype)


def q_absorb(qs, wuk_bf, g_kn, nh, kvl):
    ts = qs.shape[0]
    return pl.pallas_call(
        _q_absorb_kernel,
        grid=(nh,),
        in_specs=[pl.BlockSpec((ts, 2 * LANE), lambda h: (0, h)),
                  pl.BlockSpec((kvl, LANE), lambda h: (0, h)),
                  pl.BlockSpec((1, LANE), lambda h: (0, 0))],
        out_specs=pl.BlockSpec((ts, kvl), lambda h: (0, h)),
        out_shape=jax.ShapeDtypeStruct((ts, nh * kvl), BF16),
        compiler_params=_cparams("parallel"),
        name="q_absorb",
    )(qs, wuk_bf, g_kn.reshape(1, LANE))


def _o_up_kernel(o_ref, w_ref, out_ref):
    out_ref[...] = _dot(o_ref[...], w_ref[...]).astype(out_ref.dtype)


def o_up(o_lat, wuv_bf, nh, kvl):
    ts = o_lat.shape[0]
    return pl.pallas_call(
        _o_up_kernel,
        grid=(nh,),
        in_specs=[pl.BlockSpec((ts, kvl), lambda h: (0, h)), pl.BlockSpec((kvl, LANE), lambda h: (0, h))],
        out_specs=pl.BlockSpec((ts, LANE), lambda h: (0, h)),
        out_shape=jax.ShapeDtypeStruct((ts, nh * LANE), BF16),
        compiler_params=_cparams("parallel"),
        name="o_up",
    )(o_lat, wuv_bf)


def _decode_kernel(pt_ref, qlat_ref, qr_ref, slat_ref, skr_ref, sks_ref, *rest, pp, nh, tq, scale):
    pages = rest[:3 * pp]
    o_ref = rest[3 * pp]
    m_ref, l_ref, acc_ref = rest[3 * pp + 1:]
    c = pl.program_id(1)
    qlat = qlat_ref[0]
    qr = qr_ref[0]

    def scores(lat_b, kr_t, ks_t):
        ksx = jnp.concatenate([ks_t] * tq, axis=0)
        return (_nt(qlat, lat_b) * ksx + _dot(qr, kr_t.astype(BF16))) * scale

    @pl.when(c == 0)
    def _():
        lat_b = slat_ref[0].astype(BF16)
        s = scores(lat_b, skr_ref[0], sks_ref[0])
        u = lax.broadcasted_iota(jnp.int32, s.shape, 1)
        r = lax.broadcasted_iota(jnp.int32, s.shape, 0)
        s = jnp.where(u * nh <= r, s, -jnp.inf)
        m0 = jnp.max(s, axis=-1, keepdims=True)
        p = jnp.exp(s - m0)
        m_ref[...] = m0
        l_ref[...] = jnp.sum(p, axis=-1, keepdims=True)
        acc_ref[...] = _dot(p.astype(BF16), lat_b)

    grp = DECODE_PAGE_GROUP if pp % DECODE_PAGE_GROUP == 0 else 1
    cat = lambda kind, s0, axis: jnp.concatenate([pages[3 * s + kind][...] for s in range(s0, s0 + grp)], axis=axis)
    lats = [cat(0, s0, 0).astype(BF16) for s0 in range(0, pp, grp)]
    ss = [scores(lats[s0 // grp], cat(1, s0, 1), cat(2, s0, 1)) for s0 in range(0, pp, grp)]
    m_prev = m_ref[...]
    m_new = m_prev
    for s in ss:
        m_new = jnp.maximum(m_new, jnp.max(s, axis=-1, keepdims=True))
    alpha = jnp.exp(m_prev - m_new)
    l = l_ref[...] * alpha
    acc = acc_ref[...] * alpha
    for s, lat_b in zip(ss, lats):
        p = jnp.exp(s - m_new)
        l = l + jnp.sum(p, axis=-1, keepdims=True)
        acc = acc + _dot(p.astype(BF16), lat_b)
    m_ref[...] = m_new
    l_ref[...] = l
    acc_ref[...] = acc

    @pl.when(c == pl.num_programs(1) - 1)
    def _():
        o_ref[0] = (acc / l).astype(o_ref.dtype)


def decode_attn(qlat, qr, self_lat, self_kr_t, self_ks_t, cache_lat, cache_kr_t, cache_ks_t, page_table,
                layer, nh, tq, scale):
    bs, rows, kvl = qlat.shape
    n_pages = page_table.shape[1]
    page = cache_lat.shape[2]
    rope = cache_kr_t.shape[2]
    pp = _pick(n_pages, (32, 16, 8, 4, 2, 1))
    nc = n_pages // pp

    def per_b(shape):
        return pl.BlockSpec((1,) + shape, lambda b, c, pt: (b, 0, 0))

    def paged(shape, s):
        return pl.BlockSpec((None, None) + shape, lambda b, c, pt: (layer, pt[b * n_pages + c * pp + s], 0, 0))

    in_specs = [per_b((rows, kvl)), per_b((rows, rope)), per_b((page, kvl)), per_b((rope, page)), per_b((nh, page))]
    args = [qlat, qr, self_lat, self_kr_t, self_ks_t]
    for s in range(pp):
        in_specs += [paged((page, kvl), s), paged((rope, page), s), paged((nh, page), s)]
        args += [cache_lat, cache_kr_t, cache_ks_t]
    return pl.pallas_call(
        functools.partial(_decode_kernel, pp=pp, nh=nh, tq=tq, scale=scale),
        grid_spec=pltpu.PrefetchScalarGridSpec(
            num_scalar_prefetch=1,
            grid=(bs, nc),
            in_specs=in_specs,
            out_specs=pl.BlockSpec((1, rows, kvl), lambda b, c, pt: (b, 0, 0)),
            scratch_shapes=[pltpu.VMEM((rows, 1), F32), pltpu.VMEM((rows, 1), F32), pltpu.VMEM((rows, kvl), F32)],
        ),
        out_shape=jax.ShapeDtypeStruct((bs, rows, kvl), BF16),
        compiler_params=_cparams("parallel", "arbitrary"),
        name="decode_attn",
    )(page_table.reshape(-1), *args)


def _conv_rows(x, hist, w, bias, kf):
    row = lax.broadcasted_iota(jnp.int32, x.shape, 0)
    y = bias + x * w[kf - 1:kf]
    for s in range(1, kf):
        xs = pltpu.roll(x, s, 0)
        for t in range(s):
            xs = jnp.where(row == t, hist[kf - 1 - s + t:kf - s + t], xs)
        y = y + xs * w[kf - 1 - s:kf - s]
    return y


def _conv_prompt_kernel(*refs, kf, gate):
    if gate:
        ug, uv, hg, hv, wg, wv, bg, bv, o_ref = refs
        a = _conv_rows(ug[...], hg[0], wg[...], bg[...], kf)
        b = _conv_rows(uv[...], hv[0], wv[...], bv[...], kf)
        o_ref[...] = (_silu(a) * b).astype(o_ref.dtype)
    else:
        u, h, w, b, o_ref = refs
        o_ref[...] = _silu(_conv_rows(u[...], h[0], w[...], b[...], kf)).astype(o_ref.dtype)


def _conv_col_blocks(col0, width, gate):
    tc = _pick(math.gcd(width, col0) if col0 else width, (256, 128))
    nj = width // tc
    return tc, nj, col0 // tc, ((0, nj) if gate else (0,))


def conv_prompt(u, hist, w, bias, layer, rows_out, nb, sl, col0, width, gate, out_dtype):
    kf = w.shape[1]
    tc, nj, o0, halves = _conv_col_blocks(col0, width, gate)
    u_specs = [pl.BlockSpec((sl, tc), lambda b, j, o=o: (b, o0 + o + j)) for o in halves]
    h_specs = [pl.BlockSpec((1, kf - 1, tc), lambda b, j, o=o: (b, 0, o + j)) for o in halves]
    w_specs = [pl.BlockSpec((None, kf, tc), lambda b, j, o=o: (layer, 0, o + j)) for o in halves]
    b_specs = [pl.BlockSpec((None, 1, tc), lambda b, j, o=o: (layer, 0, o + j)) for o in halves]
    n_in = len(halves)
    return pl.pallas_call(
        functools.partial(_conv_prompt_kernel, kf=kf, gate=gate),
        grid=(nb, nj),
        in_specs=u_specs + h_specs + w_specs + b_specs,
        out_specs=pl.BlockSpec((sl, tc), lambda b, j: (b, j)),
        out_shape=jax.ShapeDtypeStruct((rows_out, width), out_dtype),
        compiler_params=_cparams("parallel", "parallel"),
        name="conv_prompt_gate" if gate else "conv_prompt",
    )(*([u] * n_in + [hist] * n_in + [w] * n_in + [bias[:, None, :]] * n_in))


def _conv_sample_kernel(*refs, kf, ss, bs, gate):
    def conv(u_ref, h_ref, w_ref, b_ref, t):
        y = b_ref[...]
        for j in range(kf):
            src = h_ref[t + j] if t + j < kf - 1 else u_ref[(t + j - (kf - 1)) * bs:(t + j - (kf - 2)) * bs, :]
            y = y + src * w_ref[j:j + 1, :]
        return y

    if gate:
        ug, uv, hg, hv, wg, wv, bg, bv, o_ref = refs
        for t in range(ss):
            o_ref[t * bs:(t + 1) * bs, :] = (_silu(conv(ug, hg, wg, bg, t)) * conv(uv, hv, wv, bv, t)).astype(o_ref.dtype)
    else:
        u, h, w, b, o_ref = refs
        for t in range(ss):
            o_ref[t * bs:(t + 1) * bs, :] = _silu(conv(u, h, w, b, t)).astype(o_ref.dtype)


def conv_sample(u, hist_t, w, bias, layer, row0, ss, bs, col0, width, gate, out_dtype):
    kf = w.shape[1]
    ts = ss * bs
    assert row0 % ts == 0
    rb = row0 // ts
    tc, nj, o0, halves = _conv_col_blocks(col0, width, gate)
    u_specs = [pl.BlockSpec((ts, tc), lambda j, o=o: (rb, o0 + o + j)) for o in halves]
    h_specs = [pl.BlockSpec((kf - 1, bs, tc), lambda j, o=o: (0, 0, o + j)) for o in halves]
    w_specs = [pl.BlockSpec((None, kf, tc), lambda j, o=o: (layer, 0, o + j)) for o in halves]
    b_specs = [pl.BlockSpec((None, 1, tc), lambda j, o=o: (layer, 0, o + j)) for o in halves]
    n_in = len(halves)
    return pl.pallas_call(
        functools.partial(_conv_sample_kernel, kf=kf, ss=ss, bs=bs, gate=gate),
        grid=(nj,),
        in_specs=u_specs + h_specs + w_specs + b_specs,
        out_specs=pl.BlockSpec((ts, tc), lambda j: (0, j)),
        out_shape=jax.ShapeDtypeStruct((ts, width), out_dtype),
        compiler_params=_cparams("parallel"),
        name="conv_sample_gate" if gate else "conv_sample",
    )(*([u] * n_in + [hist_t] * n_in + [w] * n_in + [bias[:, None, :]] * n_in))


HALO_ROWS = 16
TAIL_ROWS = 8


def _ffn_up_prompt_kernel(a_ref, halo_ref, wg_ref, wv_ref, cwg_ref, cwv_ref, bg_ref, bv_ref,
                          g_ref, tg_ref, tv_ref, *, kf, tiles_per_seq):
    i = pl.program_id(0)
    a = a_ref[...]
    halo = halo_ref[...]
    keep = jnp.where(i % tiles_per_seq == 0, 0.0, 1.0)
    tm = a.shape[0]

    def half(w_ref, cw_ref, b_ref, t_ref):
        w = w_ref[...].astype(BF16)
        u = _dot(a, w)
        hist = _dot(halo, w)[HALO_ROWS - (kf - 1):] * keep
        t_ref[...] = u[tm - TAIL_ROWS:]
        return _conv_rows(u, hist, cw_ref[...], b_ref[...], kf)

    g_ref[...] = (_silu(half(wg_ref, cwg_ref, bg_ref, tg_ref)) * half(wv_ref, cwv_ref, bv_ref, tv_ref)).astype(g_ref.dtype)


def _ffn_up_sample_kernel(a_ref, hg_ref, hv_ref, wg_ref, wv_ref, cwg_ref, cwv_ref, bg_ref, bv_ref,
                          g_ref, tg_ref, tv_ref, *, kf, ss, bs):
    a = a_ref[...]

    def half(w_ref, h_ref, cw_ref, b_ref, t_ref):
        u = _dot(a, w_ref[...].astype(BF16))
        t_ref[...] = u[(ss - (kf - 1)) * bs:]
        ys = []
        for t in range(ss):
            y = b_ref[...]
            for j in range(kf):
                src = h_ref[t + j] if t + j < kf - 1 else u[(t + j - (kf - 1)) * bs:(t + j - (kf - 2)) * bs]
                y = y + src * cw_ref[j:j + 1, :]
            ys.append(y)
        return ys

    yg = half(wg_ref, hg_ref, cwg_ref, bg_ref, tg_ref)
    yv = half(wv_ref, hv_ref, cwv_ref, bv_ref, tv_ref)
    for t in range(ss):
        g_ref[t * bs:(t + 1) * bs, :] = (_silu(yg[t]) * yv[t]).astype(g_ref.dtype)


def ffn_up_gate(h, hist_t, w_up, conv_w, conv_b, layer, bp, sp, ss, bs):
    t_all, d = h.shape
    tp, ts = bp * sp, ss * bs
    dff = w_up.shape[2] // 2
    kf = conv_w.shape[1]
    tc = _pick(dff, (256, 128))
    nj = dff // tc
    tm = _pick(sp, (1024, 512, 256, 128, 64, 32, 16))
    nt = tp // tm
    assert tm % HALO_ROWS == 0 and tp % ts == 0 and kf - 1 <= TAIL_ROWS

    def cols(shape, first_dims, o):
        return pl.BlockSpec(shape, lambda *g, o=o: first_dims + (o + g[-1],))

    def wspecs(o):
        return [cols((None, d, tc), (layer, 0), o)]

    def cspecs(o):
        return [cols((None, kf, tc), (layer, 0), o), cols((None, 1, tc), (layer, 0), o)]

    cb = conv_b[:, None, :]
    g, tg, tv = pl.pallas_call(
        functools.partial(_ffn_up_prompt_kernel, kf=kf, tiles_per_seq=sp // tm),
        grid=(nt, nj),
        in_specs=[pl.BlockSpec((tm, d), lambda i, j: (i, 0), pipeline_mode=pl.Buffered(1)),
                  pl.BlockSpec((HALO_ROWS, d), lambda i, j: (jnp.maximum(i * (tm // HALO_ROWS) - 1, 0), 0))]
                 + wspecs(0) + wspecs(nj) + [cspecs(0)[0], cspecs(nj)[0], cspecs(0)[1], cspecs(nj)[1]],
        out_specs=[pl.BlockSpec((tm, tc), lambda i, j: (i, j)),
                   pl.BlockSpec((TAIL_ROWS, tc), lambda i, j: (i, j)),
                   pl.BlockSpec((TAIL_ROWS, tc), lambda i, j: (i, j))],
        out_shape=[jax.ShapeDtypeStruct((t_all, dff), BF16),
                   jax.ShapeDtypeStruct((nt * TAIL_ROWS, dff), F32),
                   jax.ShapeDtypeStruct((nt * TAIL_ROWS, dff), F32)],
        compiler_params=_cparams("parallel", "arbitrary"),
        name="ffn_up_prompt",
    )(h, h, w_up, w_up, conv_w, conv_w, cb, cb)
    per_seq = lambda r: r.reshape(nt, TAIL_ROWS, dff)[sp // tm - 1::sp // tm, TAIL_ROWS - (kf - 1):]
    tails_p = jnp.concatenate([per_seq(tg), per_seq(tv)], axis=-1)

    hspec = lambda o: [cols((kf - 1, bs, tc), (0, 0), o)]
    g_s, sg, sv = pl.pallas_call(
        functools.partial(_ffn_up_sample_kernel, kf=kf, ss=ss, bs=bs),
        grid=(nj,),
        in_specs=[pl.BlockSpec((ts, d), lambda j: (tp // ts, 0))] + hspec(0) + hspec(nj)
                 + wspecs(0) + wspecs(nj) + [cspecs(0)[0], cspecs(nj)[0], cspecs(0)[1], cspecs(nj)[1]],
        out_specs=[pl.BlockSpec((ts, tc), lambda j: (0, j)),
                   pl.BlockSpec(((kf - 1) * bs, tc), lambda j: (0, j)),
                   pl.BlockSpec(((kf - 1) * bs, tc), lambda j: (0, j))],
        out_shape=[jax.ShapeDtypeStruct((ts, dff), BF16),
                   jax.ShapeDtypeStruct(((kf - 1) * bs, dff), F32),
                   jax.ShapeDtypeStruct(((kf - 1) * bs, dff), F32)],
        compiler_params=_cparams("arbitrary"),
        name="ffn_up_sample",
    )(h, hist_t, hist_t, w_up, w_up, conv_w, conv_w, cb, cb)
    tails_s = jnp.concatenate([sg, sv], axis=-1).reshape(kf - 1, bs, 2 * dff).swapaxes(0, 1)
    return g, g_s, tails_p, tails_s


def _gated_group_norm(y, z, gn):
    y = y * _silu(z)
    ms = jnp.sum(y * y, axis=-1, keepdims=True) * (1.0 / y.shape[-1])
    return (y * lax.rsqrt(ms + EPS)) * gn


def _ssd_prompt_kernel(xs_ref, bm_ref, cm_ref, z_ref, dt_ref, dtt_ref, bias_ref, alog_ref, dskip_ref, gn_ref,
                       biast_ref, alogt_ref, y_ref, hl_ref, h_ref, *, hp, hg):
    g = pl.program_id(1)
    c = pl.program_id(2)

    @pl.when(c == 0)
    def _():
        h_ref[...] = jnp.zeros_like(h_ref)

    xs = xs_ref[...]
    L, gw = xs.shape
    nh = dt_ref.shape[1]
    hpl = LANE // hp
    bm = bm_ref[...].astype(BF16)
    cm = cm_ref[...].astype(BF16)
    ri = lax.broadcasted_iota(jnp.int32, (L, L), 0)
    ci = lax.broadcasted_iota(jnp.int32, (L, L), 1)
    causal = ri >= ci
    dtc = _softplus(dt_ref[...] + bias_ref[...])
    tril = causal.astype(BF16)
    cumc = sum(_dot(tril, piece) for piece in _split3(dtc * (-jnp.exp(alog_ref[...]))))
    spread = (lax.broadcasted_iota(jnp.int32, (nh, gw), 0)
              == g * hg + lax.broadcasted_iota(jnp.int32, (nh, gw), 1) // hp).astype(BF16)

    def to_lanes(v):
        return sum(_dot(piece, spread) for piece in _split3(v))

    dt = to_lanes(dtc)
    cum = to_lanes(cumc)
    dtt = _softplus(dtt_ref[...] + biast_ref[...])
    triu = (ri <= ci).astype(BF16)
    cumt = sum(_dot(piece, triu) for piece in _split3(dtt * (-jnp.exp(alogt_ref[...]))))
    xdt = xs * dt
    cb = _nt(cm, bm)
    cum_last = cum[L - 1:L, :]
    xw = xdt * jnp.exp(cum_last - cum)
    lane_head = lax.broadcasted_iota(jnp.int32, (L, LANE), 1) // hp
    row_head = lax.broadcasted_iota(jnp.int32, (LANE, 1), 0) // hp
    h_all = h_ref[...]
    ys, xwts, dcols = [], [], []
    for pi in range(gw // LANE):
        sl = slice(pi * LANE, (pi + 1) * LANE)
        xdt_p = xdt[:, sl]
        ms, xks = [], []
        dcol = jnp.zeros((LANE, 1), F32)
        for q in range(hpl):
            k = pi * hpl + q
            colb = jnp.broadcast_to(cum[:, k * hp:k * hp + 1], (L, L))
            rowb = jnp.broadcast_to(cumt[k:k + 1, :], (L, L))
            dec = jnp.where(causal, jnp.exp(colb - rowb), 0.0)
            ms.append((cb * dec).astype(BF16))
            xks.append(jnp.where(lane_head == q, xdt_p, 0.0).astype(BF16))
            dcol = jnp.where(row_head == q, jnp.exp(cumt[k:k + 1, L - 1:L]), dcol)
        ys.append(_dot(jnp.concatenate(ms, axis=1), jnp.concatenate(xks, axis=0)))
        xwts.append(xw[:, sl].T)
        dcols.append(dcol)
    y = jnp.concatenate(ys, axis=1) + _nt(cm, h_all.astype(BF16)) * jnp.exp(cum) + xs * dskip_ref[...]
    xwt = jnp.concatenate(xwts, axis=0).astype(BF16)
    h_ref[...] = h_all * jnp.concatenate(dcols, axis=0) + _dot(xwt, bm)
    y_ref[...] = _gated_group_norm(y, z_ref[...], gn_ref[...]).astype(y_ref.dtype)

    @pl.when(c == pl.num_programs(2) - 1)
    def _():
        hl_ref[0] = h_ref[...]


def ssd_prompt(xbc, proj, dt_t, dt_bias, a_log, dskip_rep, gnorm, layer, rows_out, bp, sp, nh, hp, ns, ng):
    L = min(SSM_CHUNK, sp)
    nch = sp // L
    d_inner = nh * hp
    gw = d_inner // ng
    hg = nh // ng
    b0 = d_inner // ns
    dtb = (proj.shape[1] - nh) // nh
    assert nh % LANE == 0 and (proj.shape[1] - nh) % nh == 0
    rowb = lambda w, f: pl.BlockSpec((L, w), f)
    lvec = lambda w: pl.BlockSpec((None, 1, w), lambda b, g, c: (layer, 0, 0))
    gvec = pl.BlockSpec((None, 1, gw), lambda b, g, c: (layer, 0, g))
    vect = pl.BlockSpec((None, hg, 1), lambda b, g, c: (layer, g, 0))
    return pl.pallas_call(
        functools.partial(_ssd_prompt_kernel, hp=hp, hg=hg),
        grid=(bp, ng, nch),
        in_specs=[rowb(gw, lambda b, g, c: (b * nch + c, g)),
                  rowb(ns, lambda b, g, c: (b * nch + c, b0 + g)),
                  rowb(ns, lambda b, g, c: (b * nch + c, b0 + ng + g)),
                  rowb(gw, lambda b, g, c: (b * nch + c, g)),
                  rowb(nh, lambda b, g, c: (b * nch + c, dtb)),
                  pl.BlockSpec((hg, L), lambda b, g, c: (g, b * nch + c)),
                  lvec(nh), lvec(nh), gvec, gvec, vect, vect],
        out_specs=[rowb(gw, lambda b, g, c: (b * nch + c, g)),
                   pl.BlockSpec((1, gw, ns), lambda b, g, c: (b, g, 0))],
        out_shape=[jax.ShapeDtypeStruct((rows_out, d_inner), BF16),
                   jax.ShapeDtypeStruct((bp, d_inner, ns), F32)],
        scratch_shapes=[pltpu.VMEM((gw, ns), F32)],
        compiler_params=_cparams("parallel", "parallel", "arbitrary"),
        name="ssd_prompt",
    )(xbc, xbc, xbc, proj, proj, dt_t, dt_bias[:, None, :], a_log[:, None, :], dskip_rep, gnorm,
      dt_bias[:, :, None], a_log[:, :, None])


def _ssd_sample_kernel(*refs, ss, hp, ng, chained):
    if chained:
        refs = refs[1:]
    xs_ref, bm_ref, cm_ref, z_ref, dt_ref, bias_ref, alog_ref, dskip_ref, gn_ref, h0_ref, y_ref, hn_ref = refs
    lp = xs_ref.shape[1]
    gw = xs_ref.shape[2] // ng
    ns = bm_ref.shape[2] // ng
    hpl = LANE // hp
    row = lax.broadcasted_iota(jnp.int32, (lp, gw), 0)
    row_head = lax.broadcasted_iota(jnp.int32, (LANE, 1), 0) // hp

    def group(g, carry):
        cg = pl.ds(pl.multiple_of(g * gw, gw), gw)
        cn = pl.ds(pl.multiple_of(g * ns, ns), ns)
        xs = xs_ref[0, :, cg]
        bm = bm_ref[0, :, cn]
        cm = cm_ref[0, :, cn]
        dt = jnp.where(row < ss, _softplus(dt_ref[0, :, cg] + bias_ref[:, cg]), 0.0)
        dta = dt * (-jnp.exp(alog_ref[:, cg]))
        cums = [dta[0:1]]
        for l in range(1, ss):
            cums.append(cums[-1] + dta[l:l + 1])
        xdt = xs * dt
        h0 = h0_ref[0, cg, :]
        yoff = _nt(cm.astype(BF16), h0.astype(BF16))
        y = jnp.zeros((lp, gw), F32)
        cum_full = jnp.broadcast_to(cums[ss - 1], (lp, gw))
        for l in range(ss):
            acc = yoff[l:l + 1] * jnp.exp(cums[l]) + xs[l:l + 1] * dskip_ref[:, cg]
            for s in range(l + 1):
                cb = jnp.sum(cm[l:l + 1] * bm[s:s + 1], axis=-1, keepdims=True)
                acc = acc + (cb * jnp.exp(cums[l] - cums[s])) * xdt[s:s + 1]
            y = jnp.where(row == l, acc, y)
            cum_full = jnp.where(row == l, cums[l], cum_full)
        y_ref[0, :, cg] = _gated_group_norm(y, z_ref[0, :, cg], gn_ref[:, cg]).astype(y_ref.dtype)

        xw = xdt * jnp.exp(cums[ss - 1] - cum_full)
        xw_pad = jnp.concatenate([xw, jnp.zeros((LANE - lp, gw), F32)], axis=0)
        bm_pad = jnp.concatenate([bm, jnp.zeros((LANE - lp, ns), F32)], axis=0).astype(BF16)
        dlast = jnp.exp(cums[ss - 1])
        xwts, dcols = [], []
        for pi in range(gw // LANE):
            sl = slice(pi * LANE, (pi + 1) * LANE)
            dcol = jnp.zeros((LANE, 1), F32)
            for q in range(hpl):
                k = pi * hpl + q
                dcol = jnp.where(row_head == q, dlast[0:1, k * hp:k * hp + 1], dcol)
            dcols.append(dcol)
            xwts.append(xw_pad[:, sl].T)
        xwt = jnp.concatenate(xwts, axis=0).astype(BF16)
        hn_ref[0, cg, :] = h0 * jnp.concatenate(dcols, axis=0) + _dot(xwt, bm_pad)
        return carry

    lax.fori_loop(0, ng, group, 0)


def ssd_sample(xs, bm, cm, z, dt_rep, bias_rep, alog_rep, dskip_rep, gnorm, state, state_out, layer, ss, hp, ng):
    bs, lp, d_inner = xs.shape
    ns = state.shape[-1]
    act = lambda w: pl.BlockSpec((1, lp, w), lambda b: (b, 0, 0))
    vec = pl.BlockSpec((None, 1, d_inner), lambda b: (layer, 0, 0))
    st = pl.BlockSpec((None, 1, d_inner, ns), lambda b: (layer, b, 0, 0))
    chained = state_out is not None
    in_specs = [act(d_inner), act(ng * ns), act(ng * ns), act(d_inner), act(d_inner), vec, vec, vec, vec, st]
    args = [xs, bm, cm, z, dt_rep, bias_rep, alog_rep, dskip_rep, gnorm, state]
    if chained:
        in_specs = [pl.BlockSpec(memory_space=pl.ANY)] + in_specs
        args = [state_out] + args
    return pl.pallas_call(
        functools.partial(_ssd_sample_kernel, ss=ss, hp=hp, ng=ng, chained=chained),
        grid=(bs,),
        in_specs=in_specs,
        out_specs=[act(d_inner), st],
        out_shape=[jax.ShapeDtypeStruct((bs, lp, d_inner), BF16), jax.ShapeDtypeStruct(state.shape, F32)],
        input_output_aliases={0: 1} if chained else {},
        compiler_params=_cparams("parallel"),
        name="ssd_sample",
    )(*args)


def kernel(x_prompt, x_sample, cache_mla_latent, cache_mla_krope, cache_mla_kscale, page_table, state_ssm, state_ssm_conv, state_ffn_conv, norm_mix, norm_ffn, w_mla_in, g_q_lora, g_kv_lora, w_uq, w_uk, w_uv, w_mla_out, g_q_nope, g_q_rope, g_k_nope, g_k_rope, w_ssm_in, conv_w_ssm, conv_b_ssm, dt_bias, a_log, d_skip, g_ssm_norm, w_ssm_out, w_ffn_up, conv_w_ffn, conv_b_ffn, w_ffn_down):
    bp, sp, d = x_prompt.shape
    bs, ss, _ = x_sample.shape
    tp, ts = bp * sp, bs * ss
    t_all = tp + ts
    depth = norm_mix.shape[0]
    kvl, nh, nope = w_uk.shape[1:]
    vd = w_uv.shape[3]
    ql = g_q_lora.shape[1]
    rope = g_q_rope.shape[1]
    qk = nope + rope
    assert nope == LANE and vd == LANE and 2 * rope == LANE
    n_pages = page_table.shape[1]
    page = cache_mla_latent.shape[2]
    past_len = n_pages * page
    scale = 1.0 / math.sqrt(qk)
    n_ssm, _, nsh, hp, ns = state_ssm.shape
    d_inner = nsh * hp
    conv_dim = conv_w_ssm.shape[2]
    ng = (conv_dim - d_inner) // (2 * ns)
    kc = conv_w_ssm.shape[1]
    dff = w_ffn_down.shape[1]
    kf = conv_w_ffn.shape[1]
    assert ss >= kc - 1 and ss >= kf - 1 and ss <= SAMPLE_ROWS_PAD and LANE % hp == 0 and ns == LANE
    lp = SAMPLE_ROWS_PAD

    to_bt = lambda r: r.reshape((ss, bs) + r.shape[1:]).swapaxes(0, 1)
    to_tb = lambda r: r.swapaxes(0, 1).reshape((ts,) + r.shape[2:])
    put_sample = lambda buf, rows: lax.dynamic_update_slice(buf, rows, (tp, 0))
    seq_tails = lambda r, n, c0, c1: jnp.stack([r[(b + 1) * sp - n:(b + 1) * sp, c0:c1] for b in range(bp)])

    half = rope // 2
    inv = jnp.exp(-math.log(ROPE_THETA) * jnp.arange(half, dtype=F32) / half)
    pos = jnp.concatenate([jnp.tile(jnp.arange(sp), bp), jnp.repeat(past_len + jnp.arange(ss), bs)])
    ang = pos.astype(F32)[:, None] * inv[None, :]
    cos, sin = jnp.cos(ang), jnp.sin(ang)
    zt = jnp.zeros_like(cos)
    tabs = (jnp.concatenate([cos, cos, zt, zt], 1), jnp.concatenate([-sin, zt, zt, zt], 1),
            jnp.concatenate([zt, sin, zt, zt], 1))
    tabs_s = tuple(t[tp:] for t in tabs)

    def pad_lane(g):
        return jnp.pad(g, (0, LANE - g.shape[0])).reshape(1, LANE)

    cache_kr_t = jnp.swapaxes(cache_mla_krope, 2, 3)
    cache_ks_t = jnp.swapaxes(cache_mla_kscale, 2, 3)
    state4 = state_ssm.reshape(n_ssm, bs, d_inner, ns)
    dskip_rep = jnp.repeat(d_skip, hp, axis=1)[:, None, :]
    bias_rep = jnp.repeat(dt_bias, hp, axis=1)[:, None, :]
    alog_rep = jnp.repeat(a_log, hp, axis=1)[:, None, :]
    gnorm = g_ssm_norm[:, None, :]

    x = jnp.concatenate([x_prompt.reshape(tp, d), to_tb(x_sample)], axis=0)
    lat_o, kr_o, ksc_o, ssm_p_o, sconv_o, fconv_o = [], [], [], [], [], []
    state_new = None

    for i in range(depth):
        j = i // 2
        h = rmsnorm(x, norm_mix[i])
        if i % 2 == 0:
            w_in = jnp.pad(w_mla_in[j], ((0, 0), (0, (-w_mla_in.shape[2]) % 512)))
            a = matmul(h, w_in, name="mla_in")
            qa, c, krp = mla_post_a(a, g_q_lora[j], g_kv_lora[j], pad_lane(g_k_rope[j]), tabs, ql, kvl, rope)
            w_uq_pad = jnp.pad(w_uq[j].reshape(ql, nh, qk), ((0, 0), (0, 0), (0, 2 * LANE - qk))).reshape(ql, nh * 2 * LANE)
            q_raw = matmul(qa, w_uq_pad, name="mla_uq")
            g_qr_pad = pad_lane(g_q_rope[j])
            qcat = q_post(q_raw, g_q_nope[j], g_qr_pad, tabs, nope, rope, BF16)
            wuk_bf = w_uk[j].reshape(kvl, nh * nope).astype(BF16)
            wuv_bf = w_uv[j].reshape(kvl, nh * vd).astype(BF16)
            kcat, v, ksc = kv_up(c, krp, wuk_bf, wuv_bf, g_k_nope[j], nh, nope)
            o = flash_prompt(qcat, kcat, v, t_all, bp, sp, nh, scale)
            qs = q_post(q_raw[tp:], g_q_nope[j], g_qr_pad, tabs_s, nope, rope, F32)
            qlat = to_bt(q_absorb(qs, wuk_bf, g_k_nope[j], nh, kvl)).reshape(bs, ss * nh, kvl)
            qr_s = to_bt(qs).reshape(bs, ss, nh, 2 * LANE)[..., LANE:LANE + rope].reshape(bs, ss * nh, rope).astype(BF16)
            pad_keys = lambda r: jnp.pad(to_bt(r), ((0, 0), (0, page - ss), (0, 0)))
            o_lat = decode_attn(qlat, qr_s, pad_keys(c[tp:]), pad_keys(krp[tp:, :rope]).swapaxes(1, 2),
                                pad_keys(ksc[tp:, :nh]).swapaxes(1, 2), cache_mla_latent, cache_kr_t, cache_ks_t,
                                page_table, j, nh, ss, scale)
            o_s = o_up(to_tb(o_lat.reshape(bs, ss, nh * kvl)), wuv_bf, nh, kvl)
            x = matmul(put_sample(o, o_s), w_mla_out, layer=j, res=x, name="mla_out")
            lat_o.append(c)
            kr_o.append(krp[:, :rope])
            ksc_o.append(ksc[:, :nh])
        else:
            proj = matmul(h, w_ssm_in, layer=j, name="ssm_in")
            hist0 = jnp.zeros((bp, kc - 1, conv_dim), F32)
            xbc_p = conv_prompt(proj, hist0, conv_w_ssm, conv_b_ssm, j, tp, bp, sp, d_inner, conv_dim, False, F32)
            y, hl_p = ssd_prompt(xbc_p, proj, proj[:tp, d_inner + conv_dim:].T, dt_bias, a_log, dskip_rep, gnorm,
                                 j, t_all, bp, sp, nsh, hp, ns, ng)
            xbc_s = conv_sample(proj, state_ssm_conv[j].swapaxes(0, 1), conv_w_ssm, conv_b_ssm, j, tp, ss, bs,
                                d_inner, conv_dim, False, F32)
            proj_s = proj[tp:]
            padr = lambda r: jnp.pad(to_bt(r), ((0, 0), (0, lp - ss), (0, 0)))
            y_s, state_new = ssd_sample(
                padr(xbc_s[:, :d_inner]), padr(xbc_s[:, d_inner:d_inner + ng * ns]), padr(xbc_s[:, d_inner + ng * ns:]),
                padr(proj_s[:, :d_inner]), padr(jnp.repeat(proj_s[:, d_inner + conv_dim:], hp, axis=1)),
                bias_rep, alog_rep, dskip_rep, gnorm, state4, state_new, j, ss, hp, ng)
            x = matmul(put_sample(y, to_tb(y_s[:, :ss])), w_ssm_out, layer=j, res=x, name="ssm_out")
            ssm_p_o.append(hl_p.reshape(bp, nsh, hp, ns))
            sconv_o.append((seq_tails(proj, kc - 1, d_inner, d_inner + conv_dim),
                            to_bt(proj_s[:, d_inner:d_inner + conv_dim])[:, ss - (kc - 1):]))
        h = rmsnorm(x, norm_ffn[i])
        g, g_s, tails_p, tails_s = ffn_up_gate(h, state_ffn_conv[i].swapaxes(0, 1), w_ffn_up, conv_w_ffn, conv_b_ffn,
                                               i, bp, sp, ss, bs)
        x = matmul(put_sample(g, g_s), w_ffn_down, layer=i, res=x, name="ffn_down")
        fconv_o.append((tails_p, tails_s))

    def split(rows, tail):
        return rows[:tp].reshape((bp, sp) + tail), to_bt(rows[tp:])

    y_p, y_s = split(x, (d,))
    lat = [split(c, (kvl,)) for c in lat_o]
    kr = [split(c, (rope,)) for c in kr_o]
    ksc = [split(c, (nh,)) for c in ksc_o]
    stack = lambda pairs, k: jnp.stack([p[k] for p in pairs])
    return (y_p, y_s,
            stack(lat, 0), stack(kr, 0), stack(ksc, 0), jnp.stack(ssm_p_o), stack(sconv_o, 0), stack(fconv_o, 0),
            stack(lat, 1), stack(kr, 1), stack(ksc, 1), state_new.reshape(state_ssm.shape), stack(sconv_o, 1),
            stack(fconv_o, 1))
```
